```python
import jax
import jax.numpy as jnp
from jax import lax
import numpy as np

D_MODEL = 4096
BATCH = 4
SEQ = 4096
DEPTH = 4

N_META = 16
RET_HEADS = 8
RET_DK = 256
RET_DV = 256
RET_QK = RET_HEADS * RET_DK
RET_WIDTH = RET_HEADS * RET_DV
RET_CHUNK = 128
RET_PAD = RET_CHUNK - N_META
HGRN_HEADS = 16
HGRN_DK = 128
HGRN_DV = 128
HGRN_QK = HGRN_HEADS * HGRN_DK
HGRN_WIDTH = HGRN_HEADS * HGRN_DV
HGRN_CHUNK = 16
D_FF = 7168
CONV_WIDTH = 3
ROPE_BASE = 10000.0
LN_EPS = 1e-5
NORM_EPS = 1e-6
DEEPNORM_ALPHA = float((2 * DEPTH) ** 0.25)
DEEPNORM_BETA = float((8 * DEPTH) ** -0.25)
SPLIT_SIZES = (RET_QK, RET_QK, RET_WIDTH, RET_WIDTH, HGRN_QK, HGRN_QK, HGRN_WIDTH, HGRN_WIDTH, 2 * D_MODEL)
IN_COLS = sum(SPLIT_SIZES)
SPLIT_POINTS = tuple(int(s) for s in np.cumsum(SPLIT_SIZES)[:-1])

kernel_name = 'hybrid_retention_hgrn2_deepnorm'


def layer_norm(x, g, b):
    xf = x.astype(jnp.float32)
    mu = jnp.mean(xf, -1, keepdims=True)
    var = jnp.mean(jnp.square(xf - mu), -1, keepdims=True)
    y = (xf - mu) * lax.rsqrt(var + LN_EPS) * g.astype(jnp.float32) + b.astype(jnp.float32)
    return y.astype(x.dtype)


def rotary(x, pos):
    half = x.shape[-1] // 2
    inv = ROPE_BASE ** (-jnp.arange(half, dtype=jnp.float32) / half)
    ang = pos[:, None] * inv[None, :]
    cos = jnp.cos(ang)[None, :, None, :]
    sin = jnp.sin(ang)[None, :, None, :]
    x1, x2 = x[..., :half], x[..., half:]
    return jnp.concatenate([x1 * cos - x2 * sin, x1 * sin + x2 * cos], -1)


def retention_chunkwise(q, k, v):
    B, L, H, dk = q.shape
    dv = v.shape[-1]
    C = RET_CHUNK
    pad = ((0, 0), (RET_PAD, 0), (0, 0), (0, 0))
    N = (L + RET_PAD) // C
    qc = jnp.pad(q, pad).reshape(B, N, C, H, dk)
    kc = (jnp.pad(k, pad) * (dk ** -0.5)).reshape(B, N, C, H, dk)
    vc = jnp.pad(v, pad).reshape(B, N, C, H, dv)
    log_gamma = jnp.log1p(-jnp.exp2(-5.0 - jnp.arange(H, dtype=jnp.float32)))
    idx = jnp.arange(C, dtype=jnp.float32)
    diff = idx[:, None] - idx[None, :]
    decay = jnp.where(diff[None] >= 0, jnp.exp(jnp.maximum(diff, 0.0)[None] * log_gamma[:, None, None]), 0.0)
    scores = jnp.einsum('bnihd,bnjhd->bnhij', qc, kc) * decay[None, None]
    intra = jnp.einsum('bnhij,bnjhe->bnihe', scores, vc)
    w_k = jnp.exp((C - 1 - idx)[:, None] * log_gamma[None, :])
    chunk_kv = jnp.einsum('bnjhd,jh,bnjhe->nbhde', kc, w_k, vc)
    chunk_decay = jnp.exp(C * log_gamma)[None, :, None, None]

    def step(state, kv):
        return state * chunk_decay + kv, state

    _, prev = lax.scan(step, jnp.zeros((B, H, dk, dv), jnp.float32), chunk_kv)
    w_q = jnp.exp((idx + 1.0)[:, None] * log_gamma[None, :])
    inter = jnp.einsum('bnihd,ih,nbhde->bnihe', qc, w_q, prev)
    return (intra + inter).reshape(B, N * C, H, dv)[:, RET_PAD:]


def hgrn2_chunkwise(q, logf, k, v):
    B, L, H, dk = q.shape
    dv = v.shape[-1]
    C = HGRN_CHUNK
    N = L // C

    def to_chunks(a):
        return a.reshape(B, N, C, H, a.shape[-1]).transpose(1, 0, 3, 2, 4)

    qc, kc, vc = to_chunks(q), to_chunks(k), to_chunks(v)
    bc = jnp.cumsum(to_chunks(logf), axis=3)
    causal = jnp.tril(jnp.ones((C, C), dtype=bool))[None, None, :, :, None]

    def step(S, inp):
        q_, k_, v_, b_ = inp
        rel = jnp.where(causal, b_[:, :, :, None, :] - b_[:, :, None, :, :], -jnp.inf)
        A = jnp.sum(q_[:, :, :, None, :] * k_[:, :, None, :, :] * jnp.exp(rel), -1)
        o = jnp.einsum('bhij,bhje->bhie', A, v_) + jnp.einsum('bhid,bhde->bhie', q_ * jnp.exp(b_), S)
        b_last = b_[:, :, -1:, :]
        S_new = S * jnp.exp(b_last[:, :, 0, :, None]) + jnp.einsum('bhjd,bhje->bhde', k_ * jnp.exp(b_last - b_), v_)
        return S_new, o

    _, o = lax.scan(step, jnp.zeros((B, H, dk, dv), jnp.float32), (qc, kc, vc, bc))
    return o.transpose(1, 0, 3, 2, 4).reshape(B, L, H, dv)


def conv_ffn(h, w_up, conv_w, conv_b, w_down):
    L = h.shape[1]
    u = h @ w_up
    up = jnp.pad(u, ((0, 0), (CONV_WIDTH - 1, 0), (0, 0)))
    acc = conv_b
    for t in range(CONV_WIDTH):
        acc = acc + conv_w[t] * up[:, t:t + L]
    gate, val = jnp.split(acc, 2, axis=-1)
    return (jax.nn.silu(gate) * val) @ w_down


def hybrid_layer(x, pos, lb, w_in, b_gate, w_ret_out, w_hgrn_out, w_o, hgrn_norm_g,
                 ln1_g, ln1_b, w_up, conv_w, conv_b, w_down, ln2_g, ln2_b):
    B, L, _ = x.shape
    f32 = jnp.float32
    proj = x @ w_in
    rq, rk, rv, rg, hq, hf, hi, hg, gates = jnp.split(proj, SPLIT_POINTS, axis=-1)

    rq = rotary(rq.astype(f32).reshape(B, L, RET_HEADS, RET_DK), pos)
    rk = rotary(rk.astype(f32).reshape(B, L, RET_HEADS, RET_DK), pos)
    ro = retention_chunkwise(rq, rk, rv.astype(f32).reshape(B, L, RET_HEADS, RET_DV))
    mu = jnp.mean(ro, -1, keepdims=True)
    var = jnp.mean(jnp.square(ro - mu), -1, keepdims=True)
    ro = ((ro - mu) * lax.rsqrt(var + LN_EPS)).reshape(B, L, RET_WIDTH)
    ro = (ro * jax.nn.silu(rg.astype(f32))).astype(x.dtype)

    hq = jax.nn.silu(hq.astype(f32)).reshape(B, L, HGRN_HEADS, HGRN_DK)
    hf = hf.astype(f32).reshape(B, L, HGRN_HEADS, HGRN_DK)
    logf = jnp.logaddexp(jnp.log(lb), jnp.log1p(-lb) + jax.nn.log_sigmoid(hf))
    hk = -jnp.expm1(logf)
    ho = hgrn2_chunkwise(hq, logf, hk, hi.astype(f32).reshape(B, L, HGRN_HEADS, HGRN_DV))
    ho = ho * lax.rsqrt(jnp.mean(jnp.square(ho), -1, keepdims=True) + NORM_EPS)
    ho = ho.reshape(B, L, HGRN_WIDTH) * hgrn_norm_g.astype(f32)
    ho = (ho * jax.nn.silu(hg.astype(f32))).astype(x.dtype)

    g_ret, g_hgrn = jnp.split(jax.nn.sigmoid(gates + b_gate), 2, axis=-1)
    merged = g_ret * (ro @ w_ret_out) + g_hgrn * (ho @ w_hgrn_out)
    x = layer_norm(DEEPNORM_ALPHA * x + merged @ w_o, ln1_g, ln1_b)
    x = layer_norm(DEEPNORM_ALPHA * x + conv_ffn(x, w_up, conv_w, conv_b, w_down), ln2_g, ln2_b)
    return x


def setup_inputs(seed: int = 0) -> dict:
    key = jax.random.key(seed)
    ks = jax.random.split(key, 19)

    def nrm(k, shape, scale):
        return jax.random.normal(k, shape, jnp.float32) * scale

    return {
        'x': nrm(ks[0], (BATCH, SEQ, D_MODEL), 1.0),
        'meta_tokens': nrm(ks[1], (N_META, D_MODEL), 1.0),
        'emb_ln_g': 1.0 + nrm(ks[2], (D_MODEL,), 0.02),
        'emb_ln_b': nrm(ks[3], (D_MODEL,), 0.02),
        'hgrn_lb': nrm(ks[4], (DEPTH, HGRN_QK), 0.1),
        'w_in': nrm(ks[5], (DEPTH, D_MODEL, IN_COLS), D_MODEL ** -0.5),
        'b_gate': nrm(ks[6], (DEPTH, 2 * D_MODEL), 0.02),
        'w_ret_out': nrm(ks[7], (DEPTH, RET_WIDTH, D_MODEL), DEEPNORM_BETA * RET_WIDTH ** -0.5),
        'w_hgrn_out': nrm(ks[8], (DEPTH, HGRN_WIDTH, D_MODEL), DEEPNORM_BETA * HGRN_WIDTH ** -0.5),
        'w_o': nrm(ks[9], (DEPTH, D_MODEL, D_MODEL), DEEPNORM_BETA * D_MODEL ** -0.5),
        'hgrn_norm_g': 1.0 + nrm(ks[10], (DEPTH, HGRN_WIDTH), 0.02),
        'ln1_g': 1.0 + nrm(ks[11], (DEPTH, D_MODEL), 0.02),
        'ln1_b': nrm(ks[12], (DEPTH, D_MODEL), 0.02),
        'w_up': nrm(ks[13], (DEPTH, D_MODEL, 2 * D_FF), D_MODEL ** -0.5),
        'conv_w': nrm(ks[14], (DEPTH, CONV_WIDTH, 2 * D_FF), CONV_WIDTH ** -0.5),
        'conv_b': nrm(ks[15], (DEPTH, 2 * D_FF), 0.02),
        'w_down': nrm(ks[16], (DEPTH, D_FF, D_MODEL), DEEPNORM_BETA * D_FF ** -0.5),
        'ln2_g': 1.0 + nrm(ks[17], (DEPTH, D_MODEL), 0.02),
        'ln2_b': nrm(ks[18], (DEPTH, D_MODEL), 0.02),
    }


def reference(x, meta_tokens, emb_ln_g, emb_ln_b, hgrn_lb, w_in, b_gate, w_ret_out, w_hgrn_out,
              w_o, hgrn_norm_g, ln1_g, ln1_b, w_up, conv_w, conv_b, w_down, ln2_g, ln2_b):
    B = x.shape[0]
    meta = jnp.broadcast_to(meta_tokens[None].astype(x.dtype), (B, N_META, D_MODEL))
    h = layer_norm(jnp.concatenate([meta, x], axis=1), emb_ln_g, emb_ln_b)
    pos = jnp.arange(h.shape[1], dtype=jnp.float32)
    lb_all = jnp.cumsum(jax.nn.softmax(hgrn_lb.astype(jnp.float32), axis=0), axis=0)
    lb_all = lb_all - lb_all[0:1]
    for l in range(DEPTH):
        h = hybrid_layer(h, pos, lb_all[l].reshape(HGRN_HEADS, HGRN_DK), w_in[l], b_gate[l],
                         w_ret_out[l], w_hgrn_out[l], w_o[l], hgrn_norm_g[l], ln1_g[l], ln1_b[l],
                         w_up[l], conv_w[l], conv_b[l], w_down[l], ln2_g[l], ln2_b[l])
    return h[:, N_META:]
```

```python
import functools
import math

import numpy as np
import jax
import jax.numpy as jnp
from jax import lax
from jax.experimental import pallas as pl
from jax.experimental.pallas import tpu as pltpu

F32 = jnp.float32
BF16 = jnp.bfloat16

N_META = 16
RET_DK = 256
RET_CHUNK = 128
HGRN_DK = 128
HGRN_BLOCK = 128
HGRN_LEVELS = (1, 2, 4, 8, 16, 32, 64)
CONV_WIDTH = 3
ROPE_BASE = 10000.0
LN_EPS = 1e-5
NORM_EPS = 1e-6

V7X_VMEM_BYTES = 64 * 1024 * 1024
VMEM_LIMIT = V7X_VMEM_BYTES - 8 * 1024 * 1024
SUBLANES = 8
LANES = 128
BF16_ROWS = 16

NT = (((1,), (1,)), ((), ()))
TN = (((0,), (0,)), ((), ()))


def _params(n_axes):
    return pltpu.CompilerParams(dimension_semantics=("arbitrary",) * n_axes,
                                vmem_limit_bytes=VMEM_LIMIT)


def _pick_div(n, target, mult):
    best = 0
    for d in range(mult, min(n, target) + 1, mult):
        if n % d == 0:
            best = d
    assert best > 0, (n, target, mult)
    return best


def _silu(x):
    return x * jax.nn.sigmoid(x)


def _ln_kernel(x_ref, g_ref, b_ref, of_ref, ob_ref):
    x = x_ref[...]
    mu = jnp.mean(x, -1, keepdims=True)
    xc = x - mu
    var = jnp.mean(xc * xc, -1, keepdims=True)
    y = xc * lax.rsqrt(var + LN_EPS) * g_ref[...] + b_ref[...]
    of_ref[...] = y
    ob_ref[...] = y.astype(BF16)


def _layer_norm(x2d, g, b):
    rows, d = x2d.shape
    bm = _pick_div(rows, 352, BF16_ROWS)
    row_spec = pl.BlockSpec((bm, d), lambda i: (i, 0))
    vec_spec = pl.BlockSpec((1, d), lambda i: (0, 0))
    return pl.pallas_call(
        _ln_kernel,
        grid=(rows // bm,),
        in_specs=[row_spec, vec_spec, vec_spec],
        out_specs=[row_spec, row_spec],
        out_shape=[jax.ShapeDtypeStruct((rows, d), F32), jax.ShapeDtypeStruct((rows, d), BF16)],
        compiler_params=_params(1),
        name="layer_norm",
    )(x2d, g.reshape(1, d), b.reshape(1, d))


def _mm_kernel(a_ref, w_ref, o_ref):
    o_ref[...] = jnp.dot(a_ref[...], w_ref[...], preferred_element_type=F32).astype(o_ref.dtype)


def _matmul(a, w_stack, layer, bm, out_dtype):
    rows, k = a.shape
    n = w_stack.shape[-1]
    bn = _pick_div(n, 1024, LANES)
    return pl.pallas_call(
        _mm_kernel,
        grid=(rows // bm, n // bn),
        in_specs=[pl.BlockSpec((bm, k), lambda i, j: (i, 0)),
                  pl.BlockSpec((None, k, bn), lambda i, j: (layer, 0, j))],
        out_specs=pl.BlockSpec((bm, bn), lambda i, j: (i, j)),
        out_shape=jax.ShapeDtypeStruct((rows, n), out_dtype),
        compiler_params=_params(2),
        name="in_proj",
    )(a, w_stack)


def _mm_resid_kernel(a_ref, w_ref, r_ref, o_ref, *, alpha, nk):
    part = jnp.dot(a_ref[...], w_ref[...], preferred_element_type=F32)
    if nk == 1:
        o_ref[...] = alpha * r_ref[...] + part
    else:
        k = pl.program_id(2)

        @pl.when(k == 0)
        def _():
            o_ref[...] = alpha * r_ref[...] + part

        @pl.when(k > 0)
        def _():
            o_ref[...] += part


def _matmul_resid(a, w_stack, layer, resid, alpha, bm, nk):
    rows, k = a.shape
    n = w_stack.shape[-1]
    bn = _pick_div(n, 512, LANES)
    bk = k // nk
    return pl.pallas_call(
        functools.partial(_mm_resid_kernel, alpha=alpha, nk=nk),
        grid=(rows // bm, n // bn, nk),
        in_specs=[pl.BlockSpec((bm, bk), lambda i, j, kk: (i, kk)),
                  pl.BlockSpec((None, bk, bn), lambda i, j, kk: (layer, kk, j)),
                  pl.BlockSpec((bm, bn), lambda i, j, kk: (i, j))],
        out_specs=pl.BlockSpec((bm, bn), lambda i, j, kk: (i, j)),
        out_shape=jax.ShapeDtypeStruct((rows, n), F32),
        compiler_params=_params(3),
        name="proj_resid",
    )(a, w_stack, resid)


def _merge_kernel(ro_ref, ho_ref, wr_ref, wh_ref, gr_ref, gh_ref, bgr_ref, bgh_ref, o_ref):
    r = jnp.dot(ro_ref[...], wr_ref[...], preferred_element_type=F32)
    h = jnp.dot(ho_ref[...], wh_ref[...], preferred_element_type=F32)
    g_r = jax.nn.sigmoid(gr_ref[...] + bgr_ref[...])
    g_h = jax.nn.sigmoid(gh_ref[...] + bgh_ref[...])
    o_ref[...] = (g_r * r + g_h * h).astype(o_ref.dtype)


def _merge(ro, ho, wr_stack, wh_stack, layer, proj, gate_off, b_gate, bm):
    rows, rw = ro.shape
    hw = ho.shape[1]
    d = wr_stack.shape[-1]
    bn = _pick_div(math.gcd(d, gate_off), 512, LANES)
    nj = d // bn
    goff = gate_off // bn
    return pl.pallas_call(
        _merge_kernel,
        grid=(rows // bm, nj),
        in_specs=[pl.BlockSpec((bm, rw), lambda i, j: (i, 0)),
                  pl.BlockSpec((bm, hw), lambda i, j: (i, 0)),
                  pl.BlockSpec((None, rw, bn), lambda i, j: (layer, 0, j)),
                  pl.BlockSpec((None, hw, bn), lambda i, j: (layer, 0, j)),
                  pl.BlockSpec((bm, bn), lambda i, j: (i, goff + j)),
                  pl.BlockSpec((bm, bn), lambda i, j: (i, goff + nj + j)),
                  pl.BlockSpec((1, bn), lambda i, j: (0, j)),
                  pl.BlockSpec((1, bn), lambda i, j: (0, nj + j))],
        out_specs=pl.BlockSpec((bm, bn), lambda i, j: (i, j)),
        out_shape=jax.ShapeDtypeStruct((rows, d), BF16),
        compiler_params=_params(2),
        name="gated_merge",
    )(ro, ho, wr_stack, wh_stack, proj, proj, b_gate, b_gate)


def _ffn_up_kernel(x_ref, wg_ref, wv_ref, cwg_ref, cwv_ref, cbg_ref, cbv_ref, o_ref,
                   ug_ref, uv_ref, *, bm, tiles_per_batch):
    i = pl.program_id(1)

    @pl.when(i % tiles_per_batch == 0)
    def _():
        ug_ref[0:SUBLANES, :] = jnp.zeros((SUBLANES, ug_ref.shape[1]), F32)
        uv_ref[0:SUBLANES, :] = jnp.zeros((SUBLANES, uv_ref.shape[1]), F32)

    x = x_ref[...]
    ug_ref[SUBLANES:SUBLANES + bm, :] = jnp.dot(x, wg_ref[...], preferred_element_type=F32)
    uv_ref[SUBLANES:SUBLANES + bm, :] = jnp.dot(x, wv_ref[...], preferred_element_type=F32)

    def conv(u_ref, cw_ref, cb_ref):
        acc = cb_ref[...]
        for t in range(CONV_WIDTH):
            start = SUBLANES - (CONV_WIDTH - 1) + t
            acc = acc + cw_ref[t:t + 1, :] * u_ref[start:start + bm, :]
        return acc

    gate = conv(ug_ref, cwg_ref, cbg_ref)
    val = conv(uv_ref, cwv_ref, cbv_ref)
    o_ref[...] = (_silu(gate) * val).astype(o_ref.dtype)
    ug_ref[0:SUBLANES, :] = ug_ref[bm:bm + SUBLANES, :]
    uv_ref[0:SUBLANES, :] = uv_ref[bm:bm + SUBLANES, :]


def _ffn_up(x, w_stack, layer, conv_w, conv_b, bm, l_pad):
    rows, d = x.shape
    f2 = w_stack.shape[-1]
    f = f2 // 2
    bn = _pick_div(f, 512, LANES)
    nj = f // bn
    return pl.pallas_call(
        functools.partial(_ffn_up_kernel, bm=bm, tiles_per_batch=l_pad // bm),
        grid=(nj, rows // bm),
        in_specs=[pl.BlockSpec((bm, d), lambda j, i: (i, 0)),
                  pl.BlockSpec((None, d, bn), lambda j, i: (layer, 0, j)),
                  pl.BlockSpec((None, d, bn), lambda j, i: (layer, 0, nj + j)),
                  pl.BlockSpec((CONV_WIDTH, bn), lambda j, i: (0, j)),
                  pl.BlockSpec((CONV_WIDTH, bn), lambda j, i: (0, nj + j)),
                  pl.BlockSpec((1, bn), lambda j, i: (0, j)),
                  pl.BlockSpec((1, bn), lambda j, i: (0, nj + j))],
        out_specs=pl.BlockSpec((bm, bn), lambda j, i: (i, j)),
        out_shape=jax.ShapeDtypeStruct((rows, f), BF16),
        scratch_shapes=[pltpu.VMEM((bm + SUBLANES, bn), F32), pltpu.VMEM((bm + SUBLANES, bn), F32)],
        compiler_params=_params(2),
        name="ffn_up_conv_gate",
    )(x, w_stack, w_stack, conv_w, conv_w, conv_b, conv_b)


def _ret_kernel(lg_ref, q_ref, k_ref, v_ref, g_ref, cos_ref, sin_ref, o_ref, s_ref, *, n_chunks):
    c = RET_CHUNK
    half = RET_DK // 2
    lg = lg_ref[pl.program_id(1)]
    ii = lax.broadcasted_iota(jnp.int32, (c, c), 0)
    jj = lax.broadcasted_iota(jnp.int32, (c, c), 1)
    diff = (ii - jj).astype(F32)
    decay = jnp.where(diff >= 0, jnp.exp(jnp.maximum(diff, 0.0) * lg), 0.0)
    row = lax.broadcasted_iota(jnp.int32, (c, RET_DK), 0).astype(F32)
    w_q = jnp.exp((row + 1.0) * lg)
    w_k = jnp.exp((c - 1.0 - row) * lg)
    chunk_decay = jnp.exp(jnp.full((1, RET_DK), c, F32) * lg)
    scale = RET_DK ** -0.5
    s_ref[...] = jnp.zeros(s_ref.shape, F32)

    def rot(x, cos, sin):
        x1, x2 = x[:, :half], x[:, half:]
        return jnp.concatenate([x1 * cos - x2 * sin, x1 * sin + x2 * cos], axis=-1)

    def body(n, carry):
        r0 = pl.multiple_of(n * c, c)
        cos = cos_ref[pl.ds(r0, c), :]
        sin = sin_ref[pl.ds(r0, c), :]
        q = rot(q_ref[pl.ds(r0, c), :], cos, sin)
        k = rot(k_ref[pl.ds(r0, c), :], cos, sin) * scale
        v = v_ref[pl.ds(r0, c), :].astype(BF16)
        state = s_ref[...]
        scores = lax.dot_general(q.astype(BF16), k.astype(BF16), NT, preferred_element_type=F32) * decay
        o = jnp.dot(scores.astype(BF16), v, preferred_element_type=F32)
        o = o + jnp.dot((q * w_q).astype(BF16), state.astype(BF16), preferred_element_type=F32)
        s_ref[...] = state * chunk_decay + lax.dot_general(
            (k * w_k).astype(BF16), v, TN, preferred_element_type=F32)
        mu = jnp.mean(o, -1, keepdims=True)
        oc = o - mu
        var = jnp.mean(oc * oc, -1, keepdims=True)
        y = oc * lax.rsqrt(var + LN_EPS) * _silu(g_ref[pl.ds(r0, c), :])
        o_ref[pl.ds(r0, c), :] = y.astype(o_ref.dtype)
        return carry

    lax.fori_loop(0, n_chunks, body, 0)


def _retention(proj3, cos, sin, log_gamma, n_heads):
    b, l_pad, _ = proj3.shape

    def col(off):
        return pl.BlockSpec((None, l_pad, RET_DK), lambda bi, h: (bi, 0, off + h))

    tab = pl.BlockSpec((l_pad, RET_DK // 2), lambda bi, h: (0, 0))
    return pl.pallas_call(
        functools.partial(_ret_kernel, n_chunks=l_pad // RET_CHUNK),
        grid=(b, n_heads),
        in_specs=[pl.BlockSpec(memory_space=pltpu.SMEM),
                  col(0), col(n_heads), col(2 * n_heads), col(3 * n_heads), tab, tab],
        out_specs=pl.BlockSpec((None, l_pad, RET_DK), lambda bi, h: (bi, 0, h)),
        out_shape=jax.ShapeDtypeStruct((b, l_pad, n_heads * RET_DK), BF16),
        scratch_shapes=[pltpu.VMEM((RET_DK, RET_DK), F32)],
        compiler_params=_params(2),
        name="retention",
    )(log_gamma, proj3, proj3, proj3, proj3, cos, sin)


def _hgrn_constants():
    t = HGRN_BLOCK
    idx = np.arange(t)
    sums, upper, pair = [], [], []
    for s in HGRN_LEVELS:
        r = (idx // (2 * s)) * 2 * s + s - 1
        up = (idx % (2 * s)) >= s
        tt = idx[None, :]
        m = np.where(up[:, None], (tt > r[:, None]) & (tt <= idx[:, None]),
                     (tt > idx[:, None]) & (tt <= r[:, None]))
        sums.append(m)
        upper.append(np.broadcast_to(up[:, None], (t, HGRN_DK)))
        same = (idx[:, None] // (2 * s)) == (idx[None, :] // (2 * s))
        pair.append(same & up[:, None] & ~up[None, :])
    sums.append(idx[None, :] <= idx[:, None])
    sums.append(idx[None, :] > idx[:, None])
    pair.append(np.eye(t, dtype=bool))
    return (np.concatenate(sums, 0).astype(np.float32), np.stack(upper).astype(np.float32),
            np.stack(pair).astype(np.float32))


def _hgrn_kernel(q_ref, f_ref, v_ref, g_ref, loglb_ref, l1mlb_ref, ng_ref, sums_ref, upper_ref,
                 pair_ref, o_ref, st_ref, *, n_blocks):
    t = HGRN_BLOCK
    nl = len(HGRN_LEVELS)
    st_ref[...] = jnp.zeros(st_ref.shape, F32)
    log_lb = loglb_ref[...]
    log_1m_lb = l1mlb_ref[...]
    norm_g = ng_ref[...]

    def body(n, carry):
        r0 = pl.multiple_of(n * t, t)
        hf = f_ref[pl.ds(r0, t), :]
        log_sig = jnp.minimum(hf, 0.0) - jnp.log1p(jnp.exp(-jnp.abs(hf)))
        b2 = log_1m_lb + log_sig
        logf = jnp.maximum(log_lb, b2) + jnp.log1p(jnp.exp(-jnp.abs(log_lb - b2)))
        k = jnp.exp(b2 - hf)
        q = _silu(q_ref[pl.ds(r0, t), :])
        v = v_ref[pl.ds(r0, t), :].astype(BF16)

        f_hi = logf.astype(BF16)
        rem = logf - f_hi.astype(F32)
        f_mid = rem.astype(BF16)
        f_lo = (rem - f_mid.astype(F32)).astype(BF16)
        sums = sums_ref[...]
        e = (jnp.dot(sums, f_hi, preferred_element_type=F32)
             + jnp.dot(sums, f_mid, preferred_element_type=F32)
             + jnp.dot(sums, f_lo, preferred_element_type=F32))
        x = jnp.exp(e)

        qb = q.astype(BF16)
        kb = k.astype(BF16)
        a = pair_ref[nl] * lax.dot_general(qb, kb, NT, preferred_element_type=F32)
        for lvl in range(nl):
            z = (jnp.where(upper_ref[lvl] > 0.0, q, k) * x[lvl * t:(lvl + 1) * t]).astype(BF16)
            a = a + pair_ref[lvl] * lax.dot_general(z, z, NT, preferred_element_type=F32)

        q_in = (q * x[nl * t:(nl + 1) * t]).astype(BF16)
        k_out = (k * x[(nl + 1) * t:(nl + 2) * t]).astype(BF16)
        block_decay = x[(nl + 1) * t - 1:(nl + 1) * t]
        st = st_ref[...]
        o = jnp.dot(a.astype(BF16), v, preferred_element_type=F32)
        o = o + lax.dot_general(q_in, st.astype(BF16), NT, preferred_element_type=F32)
        st_ref[...] = st * block_decay + lax.dot_general(v, k_out, TN, preferred_element_type=F32)

        y = o * lax.rsqrt(jnp.mean(o * o, -1, keepdims=True) + NORM_EPS)
        y = y * norm_g * _silu(g_ref[pl.ds(r0, t), :])
        o_ref[pl.ds(r0, t), :] = y.astype(o_ref.dtype)
        return carry

    lax.fori_loop(0, n_blocks, body, 0)


def _hgrn(proj3, col_off, log_lb, log_1m_lb, norm_g, n_heads):
    b, l_pad, _ = proj3.shape
    sums, upper, pair = _hgrn_constants()
    base = col_off // HGRN_DK

    def col(off):
        return pl.BlockSpec((None, l_pad, HGRN_DK), lambda bi, h: (bi, 0, base + off + h))

    vec = pl.BlockSpec((1, HGRN_DK), lambda bi, h: (0, h))

    def const(shape):
        return pl.BlockSpec(shape, lambda bi, h: (0,) * len(shape))

    return pl.pallas_call(
        functools.partial(_hgrn_kernel, n_blocks=l_pad // HGRN_BLOCK),
        grid=(b, n_heads),
        in_specs=[col(0), col(n_heads), col(2 * n_heads), col(3 * n_heads), vec, vec, vec,
                  const(sums.shape), const(upper.shape), const(pair.shape)],
        out_specs=pl.BlockSpec((None, l_pad, HGRN_DK), lambda bi, h: (bi, 0, h)),
        out_shape=jax.ShapeDtypeStruct((b, l_pad, n_heads * HGRN_DK), BF16),
        scratch_shapes=[pltpu.VMEM((HGRN_DK, HGRN_DK), F32)],
        compiler_params=_params(2),
        name="hgrn2",
    )(proj3, proj3, proj3, proj3, log_lb, log_1m_lb, norm_g,
      jnp.asarray(sums, BF16), jnp.asarray(upper), jnp.asarray(pair))


def kernel(x, meta_tokens, emb_ln_g, emb_ln_b, hgrn_lb, w_in, b_gate, w_ret_out, w_hgrn_out, w_o,
           hgrn_norm_g, ln1_g, ln1_b, w_up, conv_w, conv_b, w_down, ln2_g, ln2_b):
    batch, seq, d = x.shape
    depth = w_in.shape[0]
    n_meta = meta_tokens.shape[0]
    l_valid = n_meta + seq
    l_pad = -(-l_valid // LANES) * LANES
    rows = batch * l_pad
    ret_w = w_ret_out.shape[1]
    hgrn_w = w_hgrn_out.shape[1]
    ret_heads = ret_w // RET_DK
    hgrn_heads = hgrn_w // HGRN_DK
    in_cols = w_in.shape[-1]
    assert in_cols == 4 * ret_w + 4 * hgrn_w + 2 * d
    alpha = float((2 * depth) ** 0.25)
    bm = _pick_div(l_pad, 1056, BF16_ROWS)
    nk_down = 2 if w_down.shape[1] % (2 * LANES) == 0 else 1

    meta = jnp.broadcast_to(meta_tokens[None].astype(x.dtype), (batch, n_meta, d))
    pad = jnp.zeros((batch, l_pad - l_valid, d), x.dtype)
    h0 = jnp.concatenate([meta, x, pad], axis=1).reshape(rows, d)
    xf, xb = _layer_norm(h0, emb_ln_g, emb_ln_b)

    pos = jnp.arange(l_pad, dtype=F32)
    half = RET_DK // 2
    inv = ROPE_BASE ** (-jnp.arange(half, dtype=F32) / half)
    ang = pos[:, None] * inv[None, :]
    cos, sin = jnp.cos(ang), jnp.sin(ang)
    log_gamma = jnp.log1p(-jnp.exp2(-5.0 - jnp.arange(ret_heads, dtype=F32)))

    lb_all = jnp.cumsum(jax.nn.softmax(hgrn_lb.astype(F32), axis=0), axis=0)
    lb_all = lb_all - lb_all[0:1]
    log_lb_all = jnp.log(lb_all)
    log_1m_lb_all = jnp.log1p(-lb_all)

    w_in_b, w_ret_b, w_hgrn_b, w_o_b, w_up_b, w_down_b = (
        w.astype(BF16) for w in (w_in, w_ret_out, w_hgrn_out, w_o, w_up, w_down))

    for layer in range(depth):
        proj = _matmul(xb, w_in_b, layer, bm, F32)
        proj3 = proj.reshape(batch, l_pad, in_cols)
        ro = _retention(proj3, cos, sin, log_gamma, ret_heads).reshape(rows, ret_w)
        ho = _hgrn(proj3, 4 * ret_w, log_lb_all[layer:layer + 1], log_1m_lb_all[layer:layer + 1],
                   hgrn_norm_g[layer:layer + 1], hgrn_heads).reshape(rows, hgrn_w)
        merged = _merge(ro, ho, w_ret_b, w_hgrn_b, layer, proj, 4 * ret_w + 4 * hgrn_w,
                        b_gate[layer:layer + 1], bm)
        z1 = _matmul_resid(merged, w_o_b, layer, xf, alpha, bm, 1)
        x1f, x1b = _layer_norm(z1, ln1_g[layer], ln1_b[layer])
        act = _ffn_up(x1b, w_up_b, layer, conv_w[layer], conv_b[layer:layer + 1], bm, l_pad)
        z2 = _matmul_resid(act, w_down_b, layer, x1f, alpha, bm, nk_down)
        xf, xb = _layer_norm(z2, ln2_g[layer], ln2_b[layer])

    return xf.reshape(batch, l_pad, d)[:, n_meta:l_valid]
```

```python
import functools
import math

import numpy as np
import jax
import jax.numpy as jnp
from jax import lax
from jax.experimental import pallas as pl
from jax.experimental.pallas import tpu as pltpu

F32 = jnp.float32
BF16 = jnp.bfloat16

N_META = 16
RET_DK = 256
RET_CHUNK = 128
HGRN_DK = 128
HGRN_BLOCK = 128
HGRN_LEVELS = (1, 2, 4, 8, 16, 32, 64)
CONV_WIDTH = 3
ROPE_BASE = 10000.0
LN_EPS = 1e-5
NORM_EPS = 1e-6

V7X_VMEM_BYTES = 64 * 1024 * 1024
VMEM_LIMIT = V7X_VMEM_BYTES - 8 * 1024 * 1024
SUBLANES = 8
LANES = 128
BF16_ROWS = 16
SEQ_ROWS_TARGET = 8192

NT = (((1,), (1,)), ((), ()))
TN = (((0,), (0,)), ((), ()))


def _params(n_axes):
    return pltpu.CompilerParams(dimension_semantics=("arbitrary",) * n_axes,
                                vmem_limit_bytes=VMEM_LIMIT)


def _pick_div(n, target, mult):
    best = 0
    for d in range(mult, min(n, target) + 1, mult):
        if n % d == 0:
            best = d
    assert best > 0, (n, target, mult)
    return best


def _unroll(trips):
    return next(u for u in (3, 2, 1) if trips % u == 0)


def _silu(x):
    h = 0.5 * x
    return h + h * jnp.tanh(h)


def _ln_kernel(x_ref, g_ref, b_ref, of_ref, ob_ref):
    x = x_ref[...]
    mu = jnp.mean(x, -1, keepdims=True)
    xc = x - mu
    var = jnp.mean(xc * xc, -1, keepdims=True)
    y = xc * lax.rsqrt(var + LN_EPS) * g_ref[...] + b_ref[...]
    of_ref[...] = y
    ob_ref[...] = y.astype(BF16)


def _layer_norm(x2d, g, b):
    rows, d = x2d.shape
    bm = _pick_div(rows, 352, BF16_ROWS)
    row_spec = pl.BlockSpec((bm, d), lambda i: (i, 0))
    vec_spec = pl.BlockSpec((1, d), lambda i: (0, 0))
    return pl.pallas_call(
        _ln_kernel,
        grid=(rows // bm,),
        in_specs=[row_spec, vec_spec, vec_spec],
        out_specs=[row_spec, row_spec],
        out_shape=[jax.ShapeDtypeStruct((rows, d), F32), jax.ShapeDtypeStruct((rows, d), BF16)],
        compiler_params=_params(1),
        name="layer_norm",
    )(x2d, g.reshape(1, d), b.reshape(1, d))


def _mm_kernel(a_ref, w_ref, o_ref):
    o_ref[...] = jnp.dot(a_ref[...], w_ref[...], preferred_element_type=F32).astype(o_ref.dtype)


def _matmul(a, w_stack, layer, bm, out_dtype):
    rows, k = a.shape
    n = w_stack.shape[-1]
    bn = _pick_div(n, 1024, LANES)
    return pl.pallas_call(
        _mm_kernel,
        grid=(rows // bm, n // bn),
        in_specs=[pl.BlockSpec((bm, k), lambda i, j: (i, 0)),
                  pl.BlockSpec((None, k, bn), lambda i, j: (layer, 0, j))],
        out_specs=pl.BlockSpec((bm, bn), lambda i, j: (i, j)),
        out_shape=jax.ShapeDtypeStruct((rows, n), out_dtype),
        compiler_params=_params(2),
        name="in_proj",
    )(a, w_stack)


def _mm_resid_kernel(a_ref, w_ref, r_ref, o_ref, *, alpha, nk):
    part = jnp.dot(a_ref[...], w_ref[...], preferred_element_type=F32)
    if nk == 1:
        o_ref[...] = alpha * r_ref[...] + part
    else:
        k = pl.program_id(2)

        @pl.when(k == 0)
        def _():
            o_ref[...] = alpha * r_ref[...] + part

        @pl.when(k > 0)
        def _():
            o_ref[...] += part


def _matmul_resid(a, w_stack, layer, resid, alpha, bm, nk):
    rows, k = a.shape
    n = w_stack.shape[-1]
    bn = _pick_div(n, 512, LANES)
    bk = k // nk
    return pl.pallas_call(
        functools.partial(_mm_resid_kernel, alpha=alpha, nk=nk),
        grid=(rows // bm, n // bn, nk),
        in_specs=[pl.BlockSpec((bm, bk), lambda i, j, kk: (i, kk)),
                  pl.BlockSpec((None, bk, bn), lambda i, j, kk: (layer, kk, j)),
                  pl.BlockSpec((bm, bn), lambda i, j, kk: (i, j))],
        out_specs=pl.BlockSpec((bm, bn), lambda i, j, kk: (i, j)),
        out_shape=jax.ShapeDtypeStruct((rows, n), F32),
        compiler_params=_params(3),
        name="proj_resid",
    )(a, w_stack, resid)


def _merge_kernel(ro_ref, ho_ref, wr_ref, wh_ref, gr_ref, gh_ref, bgr_ref, bgh_ref, o_ref):
    r = jnp.dot(ro_ref[...], wr_ref[...], preferred_element_type=F32)
    h = jnp.dot(ho_ref[...], wh_ref[...], preferred_element_type=F32)
    g_r = jax.nn.sigmoid(gr_ref[...].astype(F32) + bgr_ref[...])
    g_h = jax.nn.sigmoid(gh_ref[...].astype(F32) + bgh_ref[...])
    o_ref[...] = (g_r * r + g_h * h).astype(o_ref.dtype)


def _merge(ro, ho, wr_stack, wh_stack, layer, proj, gate_off, b_gate, bm):
    rows, rw = ro.shape
    hw = ho.shape[1]
    d = wr_stack.shape[-1]
    bn = _pick_div(math.gcd(d, gate_off), 512, LANES)
    nj = d // bn
    goff = gate_off // bn
    return pl.pallas_call(
        _merge_kernel,
        grid=(rows // bm, nj),
        in_specs=[pl.BlockSpec((bm, rw), lambda i, j: (i, 0)),
                  pl.BlockSpec((bm, hw), lambda i, j: (i, 0)),
                  pl.BlockSpec((None, rw, bn), lambda i, j: (layer, 0, j)),
                  pl.BlockSpec((None, hw, bn), lambda i, j: (layer, 0, j)),
                  pl.BlockSpec((bm, bn), lambda i, j: (i, goff + j)),
                  pl.BlockSpec((bm, bn), lambda i, j: (i, goff + nj + j)),
                  pl.BlockSpec((1, bn), lambda i, j: (0, j)),
                  pl.BlockSpec((1, bn), lambda i, j: (0, nj + j))],
        out_specs=pl.BlockSpec((bm, bn), lambda i, j: (i, j)),
        out_shape=jax.ShapeDtypeStruct((rows, d), BF16),
        compiler_params=_params(2),
        name="gated_merge",
    )(ro, ho, wr_stack, wh_stack, proj, proj, b_gate, b_gate)


def _ffn_up_kernel(x_ref, wg_ref, wv_ref, cwg_ref, cwv_ref, cbg_ref, cbv_ref, o_ref,
                   ug_ref, uv_ref, *, bm, tiles_per_batch):
    i = pl.program_id(1)

    @pl.when(i % tiles_per_batch == 0)
    def _():
        ug_ref[0:SUBLANES, :] = jnp.zeros((SUBLANES, ug_ref.shape[1]), F32)
        uv_ref[0:SUBLANES, :] = jnp.zeros((SUBLANES, uv_ref.shape[1]), F32)

    x = x_ref[...]
    ug_ref[SUBLANES:SUBLANES + bm, :] = jnp.dot(x, wg_ref[...], preferred_element_type=F32)
    uv_ref[SUBLANES:SUBLANES + bm, :] = jnp.dot(x, wv_ref[...], preferred_element_type=F32)

    def conv(u_ref, cw_ref, cb_ref):
        acc = cb_ref[...]
        for t in range(CONV_WIDTH):
            start = SUBLANES - (CONV_WIDTH - 1) + t
            acc = acc + cw_ref[t:t + 1, :] * u_ref[start:start + bm, :]
        return acc

    gate = conv(ug_ref, cwg_ref, cbg_ref)
    val = conv(uv_ref, cwv_ref, cbv_ref)
    o_ref[...] = (_silu(gate) * val).astype(o_ref.dtype)
    ug_ref[0:SUBLANES, :] = ug_ref[bm:bm + SUBLANES, :]
    uv_ref[0:SUBLANES, :] = uv_ref[bm:bm + SUBLANES, :]


def _ffn_up(x, w_stack, layer, conv_w, conv_b, bm, l_pad):
    rows, d = x.shape
    f2 = w_stack.shape[-1]
    f = f2 // 2
    bn = _pick_div(f, 512, LANES)
    nj = f // bn
    return pl.pallas_call(
        functools.partial(_ffn_up_kernel, bm=bm, tiles_per_batch=l_pad // bm),
        grid=(nj, rows // bm),
        in_specs=[pl.BlockSpec((bm, d), lambda j, i: (i, 0)),
                  pl.BlockSpec((None, d, bn), lambda j, i: (layer, 0, j)),
                  pl.BlockSpec((None, d, bn), lambda j, i: (layer, 0, nj + j)),
                  pl.BlockSpec((CONV_WIDTH, bn), lambda j, i: (0, j)),
                  pl.BlockSpec((CONV_WIDTH, bn), lambda j, i: (0, nj + j)),
                  pl.BlockSpec((1, bn), lambda j, i: (0, j)),
                  pl.BlockSpec((1, bn), lambda j, i: (0, nj + j))],
        out_specs=pl.BlockSpec((bm, bn), lambda j, i: (i, j)),
        out_shape=jax.ShapeDtypeStruct((rows, f), BF16),
        scratch_shapes=[pltpu.VMEM((bm + SUBLANES, bn), F32), pltpu.VMEM((bm + SUBLANES, bn), F32)],
        compiler_params=_params(2),
        name="ffn_up_conv_gate",
    )(x, w_stack, w_stack, conv_w, conv_w, conv_b, conv_b)


def _ret_kernel(lg_ref, q_ref, k_ref, v_ref, g_ref, cos_ref, sin_ref, o_ref, s_ref, *, n_chunks, heads):
    c = RET_CHUNK
    dk = RET_DK
    half = dk // 2
    scale = dk ** -0.5

    @pl.when(pl.program_id(2) == 0)
    def _():
        s_ref[...] = jnp.zeros(s_ref.shape, F32)

    ii = lax.broadcasted_iota(jnp.int32, (c, c), 0)
    jj = lax.broadcasted_iota(jnp.int32, (c, c), 1)
    diff = (ii - jj).astype(F32)
    row = lax.broadcasted_iota(jnp.int32, (c, dk), 0).astype(F32)
    decay, w_q, w_k, chunk_decay = [], [], [], []
    for g in range(heads):
        lg = lg_ref[pl.program_id(1) * heads + g]
        decay.append(jnp.where(diff >= 0, jnp.exp(jnp.maximum(diff, 0.0) * lg), 0.0))
        w_q.append(jnp.exp((row + 1.0) * lg))
        w_k.append(jnp.exp((c - 1.0 - row) * lg))
        chunk_decay.append(jnp.exp(jnp.full((1, dk), c, F32) * lg))

    def rot(x, cos, sin):
        x1, x2 = x[:, :half], x[:, half:]
        return jnp.concatenate([x1 * cos - x2 * sin, x1 * sin + x2 * cos], axis=-1)

    def body(n, carry):
        r0 = pl.multiple_of(n * c, c)
        rows = pl.ds(r0, c)
        cos = cos_ref[rows, :]
        sin = sin_ref[rows, :]
        for g in range(heads):
            cols = slice(g * dk, (g + 1) * dk)
            q = rot(q_ref[rows, cols].astype(F32), cos, sin)
            k = rot(k_ref[rows, cols].astype(F32), cos, sin) * scale
            v = v_ref[rows, cols].astype(BF16)
            state = s_ref[g]
            scores = lax.dot_general(q.astype(BF16), k.astype(BF16), NT,
                                     preferred_element_type=F32) * decay[g]
            o = jnp.dot(scores.astype(BF16), v, preferred_element_type=F32)
            o = o + jnp.dot((q * w_q[g]).astype(BF16), state.astype(BF16), preferred_element_type=F32)
            s_ref[g] = state * chunk_decay[g] + lax.dot_general(
                (k * w_k[g]).astype(BF16), v, TN, preferred_element_type=F32)
            mu = jnp.mean(o, -1, keepdims=True)
            oc = o - mu
            var = jnp.mean(oc * oc, -1, keepdims=True)
            y = oc * lax.rsqrt(var + LN_EPS) * _silu(g_ref[rows, cols].astype(F32))
            o_ref[rows, cols] = y.astype(o_ref.dtype)
        return carry

    lax.fori_loop(0, n_chunks, body, 0, unroll=_unroll(n_chunks))


def _retention(proj3, cos, sin, log_gamma, n_heads):
    b, l_pad, _ = proj3.shape
    heads = 2 if n_heads % 2 == 0 else 1
    groups = n_heads // heads
    width = heads * RET_DK
    rb = _pick_div(l_pad, SEQ_ROWS_TARGET, RET_CHUNK)

    def col(section):
        return pl.BlockSpec((None, rb, width), lambda bi, hg, r: (bi, r, section * groups + hg))

    tab = pl.BlockSpec((rb, RET_DK // 2), lambda bi, hg, r: (r, 0))
    return pl.pallas_call(
        functools.partial(_ret_kernel, n_chunks=rb // RET_CHUNK, heads=heads),
        grid=(b, groups, l_pad // rb),
        in_specs=[pl.BlockSpec(memory_space=pltpu.SMEM), col(0), col(1), col(2), col(3), tab, tab],
        out_specs=pl.BlockSpec((None, rb, width), lambda bi, hg, r: (bi, r, hg)),
        out_shape=jax.ShapeDtypeStruct((b, l_pad, n_heads * RET_DK), BF16),
        scratch_shapes=[pltpu.VMEM((heads, RET_DK, RET_DK), F32)],
        compiler_params=_params(3),
        name="retention",
    )(log_gamma, proj3, proj3, proj3, proj3, cos, sin)


def _hgrn_constants():
    t = HGRN_BLOCK
    idx = np.arange(t)
    sums, upper, pair = [], [], []
    for s in HGRN_LEVELS:
        r = (idx // (2 * s)) * 2 * s + s - 1
        up = (idx % (2 * s)) >= s
        tt = idx[None, :]
        if s < SUBLANES:
            sums.append(np.where(up[:, None], (tt > r[:, None]) & (tt <= idx[:, None]),
                                 (tt > idx[:, None]) & (tt <= r[:, None])))
        upper.append(np.broadcast_to(up[:, None], (t, HGRN_DK)))
        same = (idx[:, None] // (2 * s)) == (idx[None, :] // (2 * s))
        pair.append(same & up[:, None] & ~up[None, :])
    sums.append(idx[None, :] <= idx[:, None])
    pair.append(np.eye(t, dtype=bool))
    return (np.tile(np.concatenate(sums, 0), (1, 3)).astype(np.float32), np.stack(upper).astype(np.float32),
            np.tile(np.stack(pair), (1, 1, 2)).astype(np.float32))


def _hgrn_kernel(q_ref, f_ref, v_ref, g_ref, loglb_ref, l1mlb_ref, ng_ref, sums_ref, upper_ref,
                 pair_ref, o_ref, st_ref, *, n_blocks, heads):
    t = HGRN_BLOCK
    dk = HGRN_DK
    nl = len(HGRN_LEVELS)
    n_mxu = sum(1 for s in HGRN_LEVELS if s < SUBLANES)

    @pl.when(pl.program_id(2) == 0)
    def _():
        st_ref[...] = jnp.zeros(st_ref.shape, F32)

    log_lb = loglb_ref[...]
    log_1m_lb = l1mlb_ref[...]
    norm_g = ng_ref[...]

    def prepare(n):
        rows = pl.ds(pl.multiple_of(n * t, t), t)
        hf = f_ref[rows, :].astype(F32)
        log_sig = jnp.minimum(hf, 0.0) - jnp.log(1.0 + jnp.exp(-jnp.abs(hf)))
        b2 = log_1m_lb + log_sig
        logf = jnp.maximum(log_lb, b2) + jnp.log(1.0 + jnp.exp(-jnp.abs(log_lb - b2)))
        k_all = jnp.exp(b2 - hf)
        q_all = _silu(q_ref[rows, :].astype(F32))
        gate_all = norm_g * _silu(g_ref[rows, :].astype(F32))
        v_all = v_ref[rows, :].astype(BF16)

        f_hi = logf.astype(BF16)
        rem = logf - f_hi.astype(F32)
        f_mid = rem.astype(BF16)
        f_lo = (rem - f_mid.astype(F32)).astype(BF16)
        e_all = jnp.dot(sums_ref[...], jnp.concatenate([f_hi, f_mid, f_lo], axis=0),
                        preferred_element_type=F32)

        z_out = [[None] * heads for _ in range(nl)]
        qk_out = [[None] * heads for _ in range(4)]
        dec_out = [None] * heads
        for g in range(heads):
            cols = slice(g * dk, (g + 1) * dk)
            q, k = q_all[:, cols], k_all[:, cols]
            prefix = e_all[n_mxu * t:(n_mxu + 1) * t, cols]
            for lvl, s in enumerate(HGRN_LEVELS):
                if lvl < n_mxu:
                    ex = e_all[lvl * t:(lvl + 1) * t, cols]
                    src = jnp.where(upper_ref[lvl] > 0.0, q, k)
                else:
                    ex_parts, src_parts = [], []
                    for r0 in range(0, t, 2 * s):
                        lo, hi = slice(r0, r0 + s), slice(r0 + s, r0 + 2 * s)
                        bound = prefix[r0 + s - 1:r0 + s, :]
                        ex_parts += [bound - prefix[lo, :], prefix[hi, :] - bound]
                        src_parts += [k[lo, :], q[hi, :]]
                    ex = jnp.concatenate(ex_parts, axis=0)
                    src = jnp.concatenate(src_parts, axis=0)
                z_out[lvl][g] = (src * jnp.exp(ex)).astype(BF16)

            total = prefix[t - 1:t, :]
            qk_out[0][g] = q.astype(BF16)
            qk_out[1][g] = k.astype(BF16)
            qk_out[2][g] = (q * jnp.exp(prefix)).astype(BF16)
            qk_out[3][g] = (k * jnp.exp(total - prefix)).astype(BF16)
            dec_out[g] = jnp.exp(total)

        def cat(parts):
            return jnp.concatenate(parts, axis=1)

        return (tuple(cat(zl) for zl in z_out), tuple(cat(x) for x in qk_out), v_all, gate_all, cat(dec_out))

    zero_tile = jnp.zeros((t, dk), BF16)

    def block_diag(x):
        return jnp.concatenate([jnp.concatenate([x[:, :dk], zero_tile], axis=1),
                                jnp.concatenate([zero_tile, x[:, dk:]], axis=1)], axis=0)

    def consume(n, prepared):
        z_all, qk_all, vb_all, gate_all, dec_all = prepared
        rows = pl.ds(pl.multiple_of(n * t, t), t)
        for p in range(heads // 2):
            cols2 = slice(2 * p * dk, 2 * (p + 1) * dk)
            a = pair_ref[nl] * lax.dot_general(qk_all[0][:, cols2], block_diag(qk_all[1][:, cols2]), NT,
                                               preferred_element_type=F32)
            for lvl in range(nl):
                z = z_all[lvl][:, cols2]
                a = a + pair_ref[lvl] * lax.dot_general(z, block_diag(z), NT, preferred_element_type=F32)
            v = vb_all[:, cols2]
            st = [st_ref[2 * p], st_ref[2 * p + 1]]
            st_pair = jnp.concatenate(st, axis=1).astype(BF16)
            o = jnp.dot(a.astype(BF16), block_diag(v), preferred_element_type=F32)
            o = o + lax.dot_general(qk_all[2][:, cols2], block_diag(st_pair), NT, preferred_element_type=F32)
            for h in range(2):
                cols = slice((2 * p + h) * dk, (2 * p + h + 1) * dk)
                half = slice(h * dk, (h + 1) * dk)
                st_ref[2 * p + h] = st[h] * dec_all[:, cols] + lax.dot_general(
                    v[:, half], qk_all[3][:, cols], TN, preferred_element_type=F32)
                oh = o[:, half]
                y = oh * lax.rsqrt(jnp.mean(oh * oh, -1, keepdims=True) + NORM_EPS)
                o_ref[rows, cols] = (y * gate_all[:, cols]).astype(o_ref.dtype)

    def body(n, prepared):
        consume(n, prepared)
        return prepare(jnp.minimum(n + 1, n_blocks - 1))

    lax.fori_loop(0, n_blocks, body, prepare(0))


def _hgrn(proj3, col_off, log_lb, log_1m_lb, norm_g, n_heads):
    b, l_pad, _ = proj3.shape
    sums, upper, pair = _hgrn_constants()
    heads = next(h for h in (4, 2) if n_heads % h == 0)
    groups = n_heads // heads
    width = heads * HGRN_DK
    base = col_off // width
    rb = _pick_div(l_pad, SEQ_ROWS_TARGET, HGRN_BLOCK)

    def col(section):
        return pl.BlockSpec((None, rb, width), lambda bi, hg, r: (bi, r, base + section * groups + hg))

    vec = pl.BlockSpec((1, width), lambda bi, hg, r: (0, hg))

    def const(shape):
        return pl.BlockSpec(shape, lambda bi, hg, r: (0,) * len(shape))

    return pl.pallas_call(
        functools.partial(_hgrn_kernel, n_blocks=rb // HGRN_BLOCK, heads=heads),
        grid=(b, groups, l_pad // rb),
        in_specs=[col(0), col(1), col(2), col(3), vec, vec, vec,
                  const(sums.shape), const(upper.shape), const(pair.shape)],
        out_specs=pl.BlockSpec((None, rb, width), lambda bi, hg, r: (bi, r, hg)),
        out_shape=jax.ShapeDtypeStruct((b, l_pad, n_heads * HGRN_DK), BF16),
        scratch_shapes=[pltpu.VMEM((heads, HGRN_DK, HGRN_DK), F32)],
        compiler_params=_params(3),
        name="hgrn2",
    )(proj3, proj3, proj3, proj3, log_lb, log_1m_lb, norm_g,
      jnp.asarray(sums, BF16), jnp.asarray(upper), jnp.asarray(pair))


def kernel(x, meta_tokens, emb_ln_g, emb_ln_b, hgrn_lb, w_in, b_gate, w_ret_out, w_hgrn_out, w_o,
           hgrn_norm_g, ln1_g, ln1_b, w_up, conv_w, conv_b, w_down, ln2_g, ln2_b):
    batch, seq, d = x.shape
    depth = w_in.shape[0]
    n_meta = meta_tokens.shape[0]
    l_valid = n_meta + seq
    l_pad = -(-l_valid // LANES) * LANES
    rows = batch * l_pad
    ret_w = w_ret_out.shape[1]
    hgrn_w = w_hgrn_out.shape[1]
    ret_heads = ret_w // RET_DK
    hgrn_heads = hgrn_w // HGRN_DK
    in_cols = w_in.shape[-1]
    assert in_cols == 4 * ret_w + 4 * hgrn_w + 2 * d
    alpha = float((2 * depth) ** 0.25)
    bm = _pick_div(l_pad, 1056, BF16_ROWS)
    nk_down = 2 if w_down.shape[1] % (2 * LANES) == 0 else 1

    meta = jnp.broadcast_to(meta_tokens[None].astype(x.dtype), (batch, n_meta, d))
    pad = jnp.zeros((batch, l_pad - l_valid, d), x.dtype)
    h0 = jnp.concatenate([meta, x, pad], axis=1).reshape(rows, d)
    xf, xb = _layer_norm(h0, emb_ln_g, emb_ln_b)

    pos = jnp.arange(l_pad, dtype=F32)
    half = RET_DK // 2
    inv = ROPE_BASE ** (-jnp.arange(half, dtype=F32) / half)
    ang = pos[:, None] * inv[None, :]
    cos, sin = jnp.cos(ang), jnp.sin(ang)
    log_gamma = jnp.log1p(-jnp.exp2(-5.0 - jnp.arange(ret_heads, dtype=F32)))

    lb_all = jnp.cumsum(jax.nn.softmax(hgrn_lb.astype(F32), axis=0), axis=0)
    lb_all = lb_all - lb_all[0:1]
    log_lb_all = jnp.log(lb_all)
    log_1m_lb_all = jnp.log1p(-lb_all)

    w_in_b, w_ret_b, w_hgrn_b, w_o_b, w_up_b, w_down_b = (
        w.astype(BF16) for w in (w_in, w_ret_out, w_hgrn_out, w_o, w_up, w_down))

    for layer in range(depth):
        proj = _matmul(xb, w_in_b, layer, bm, BF16)
        proj3 = proj.reshape(batch, l_pad, in_cols)
        ro = _retention(proj3, cos, sin, log_gamma, ret_heads).reshape(rows, ret_w)
        ho = _hgrn(proj3, 4 * ret_w, log_lb_all[layer:layer + 1], log_1m_lb_all[layer:layer + 1],
                   hgrn_norm_g[layer:layer + 1], hgrn_heads).reshape(rows, hgrn_w)
        merged = _merge(ro, ho, w_ret_b, w_hgrn_b, layer, proj, 4 * ret_w + 4 * hgrn_w,
                        b_gate[layer:layer + 1], bm)
        z1 = _matmul_resid(merged, w_o_b, layer, xf, alpha, bm, 1)
        x1f, x1b = _layer_norm(z1, ln1_g[layer], ln1_b[layer])
        act = _ffn_up(x1b, w_up_b, layer, conv_w[layer], conv_b[layer:layer + 1], bm, l_pad)
        z2 = _matmul_resid(act, w_down_b, layer, x1f, alpha, bm, nk_down)
        xf, xb = _layer_norm(z2, ln2_g[layer], ln2_b[layer])

    return xf.reshape(batch, l_pad, d)[:, n_meta:l_valid]
```

```python
import functools
import math

import numpy as np
import jax
import jax.numpy as jnp
from jax import lax
from jax.experimental import pallas as pl
from jax.experimental.pallas import tpu as pltpu

F32 = jnp.float32
BF16 = jnp.bfloat16

N_META = 16
RET_DK = 256
RET_CHUNK = 128
HGRN_DK = 128
HGRN_BLOCK = 128
HGRN_LEVELS = (1, 2, 4, 8, 16, 32, 64)
CONV_WIDTH = 3
ROPE_BASE = 10000.0
LN_EPS = 1e-5
NORM_EPS = 1e-6

V7X_VMEM_BYTES = 64 * 1024 * 1024
VMEM_LIMIT = V7X_VMEM_BYTES - 8 * 1024 * 1024
SUBLANES = 8
LANES = 128
MXU_COLS = 256
BF16_ROWS = 16
SEQ_ROWS_TARGET = 8192

NT = (((1,), (1,)), ((), ()))
TN = (((0,), (0,)), ((), ()))


def _params(n_axes):
    return pltpu.CompilerParams(dimension_semantics=("arbitrary",) * n_axes,
                                vmem_limit_bytes=VMEM_LIMIT)


def _pick_div(n, target, mult):
    best = 0
    for d in range(mult, min(n, target) + 1, mult):
        if n % d == 0:
            best = d
    assert best > 0, (n, target, mult)
    return best


def _unroll(trips):
    return next(u for u in (3, 2, 1) if trips % u == 0)


def _sigmoid(x):
    return 0.5 + 0.5 * jnp.tanh(0.5 * x)


def _silu(x):
    h = 0.5 * x
    return h + h * jnp.tanh(h)


def _ln_kernel(x_ref, g_ref, b_ref, of_ref, ob_ref):
    x = x_ref[...]
    mu = jnp.mean(x, -1, keepdims=True)
    xc = x - mu
    var = jnp.mean(xc * xc, -1, keepdims=True)
    y = xc * lax.rsqrt(var + LN_EPS) * g_ref[...] + b_ref[...]
    of_ref[...] = y
    ob_ref[...] = y.astype(BF16)


def _layer_norm(x2d, g, b):
    rows, d = x2d.shape
    bm = _pick_div(rows, 352, BF16_ROWS)
    row_spec = pl.BlockSpec((bm, d), lambda i: (i, 0))
    vec_spec = pl.BlockSpec((1, d), lambda i: (0, 0))
    return pl.pallas_call(
        _ln_kernel,
        grid=(rows // bm,),
        in_specs=[row_spec, vec_spec, vec_spec],
        out_specs=[row_spec, row_spec],
        out_shape=[jax.ShapeDtypeStruct((rows, d), F32), jax.ShapeDtypeStruct((rows, d), BF16)],
        compiler_params=_params(1),
        name="layer_norm",
    )(x2d, g.reshape(1, d), b.reshape(1, d))


def _mm_kernel(a_ref, w_ref, o_ref):
    o_ref[...] = jnp.dot(a_ref[...], w_ref[...], preferred_element_type=F32).astype(o_ref.dtype)


def _matmul(a, w_stack, layer, bm, out_dtype):
    rows, k = a.shape
    n = w_stack.shape[-1]
    bn = _pick_div(n, 1024, LANES)
    return pl.pallas_call(
        _mm_kernel,
        grid=(rows // bm, n // bn),
        in_specs=[pl.BlockSpec((bm, k), lambda i, j: (i, 0)),
                  pl.BlockSpec((None, k, bn), lambda i, j: (layer, 0, j))],
        out_specs=pl.BlockSpec((bm, bn), lambda i, j: (i, j)),
        out_shape=jax.ShapeDtypeStruct((rows, n), out_dtype),
        compiler_params=_params(2),
        name="in_proj",
    )(a, w_stack)


def _mm_resid_kernel(a_ref, w_ref, r_ref, o_ref, *, alpha, nk):
    part = jnp.dot(a_ref[...], w_ref[...], preferred_element_type=F32)
    if nk == 1:
        o_ref[...] = alpha * r_ref[...] + part
    else:
        k = pl.program_id(2)

        @pl.when(k == 0)
        def _():
            o_ref[...] = alpha * r_ref[...] + part

        @pl.when(k > 0)
        def _():
            o_ref[...] += part


def _matmul_resid(a, w_stack, layer, resid, alpha, bm, nk):
    rows, k = a.shape
    n = w_stack.shape[-1]
    bn = _pick_div(n, 512, LANES)
    bk = k // nk
    return pl.pallas_call(
        functools.partial(_mm_resid_kernel, alpha=alpha, nk=nk),
        grid=(rows // bm, n // bn, nk),
        in_specs=[pl.BlockSpec((bm, bk), lambda i, j, kk: (i, kk)),
                  pl.BlockSpec((None, bk, bn), lambda i, j, kk: (layer, kk, j)),
                  pl.BlockSpec((bm, bn), lambda i, j, kk: (i, j))],
        out_specs=pl.BlockSpec((bm, bn), lambda i, j, kk: (i, j)),
        out_shape=jax.ShapeDtypeStruct((rows, n), F32),
        compiler_params=_params(3),
        name="proj_resid",
    )(a, w_stack, resid)


def _merge_kernel(ro_ref, ho_ref, wr_ref, wh_ref, gr_ref, gh_ref, bgr_ref, bgh_ref, o_ref):
    r = jnp.dot(ro_ref[...], wr_ref[...], preferred_element_type=F32)
    h = jnp.dot(ho_ref[...], wh_ref[...], preferred_element_type=F32)
    g_r = _sigmoid(gr_ref[...].astype(F32) + bgr_ref[...])
    g_h = _sigmoid(gh_ref[...].astype(F32) + bgh_ref[...])
    o_ref[...] = (g_r * r + g_h * h).astype(o_ref.dtype)


def _merge(ro, ho, wr_stack, wh_stack, layer, proj, gate_off, b_gate, bm):
    rows, rw = ro.shape
    hw = ho.shape[1]
    d = wr_stack.shape[-1]
    bn = _pick_div(math.gcd(d, gate_off), 512, MXU_COLS)
    nj = d // bn
    goff = gate_off // bn
    return pl.pallas_call(
        _merge_kernel,
        grid=(rows // bm, nj),
        in_specs=[pl.BlockSpec((bm, rw), lambda i, j: (i, 0)),
                  pl.BlockSpec((bm, hw), lambda i, j: (i, 0)),
                  pl.BlockSpec((None, rw, bn), lambda i, j: (layer, 0, j)),
                  pl.BlockSpec((None, hw, bn), lambda i, j: (layer, 0, j)),
                  pl.BlockSpec((bm, bn), lambda i, j: (i, goff + j)),
                  pl.BlockSpec((bm, bn), lambda i, j: (i, goff + nj + j)),
                  pl.BlockSpec((1, bn), lambda i, j: (0, j)),
                  pl.BlockSpec((1, bn), lambda i, j: (0, nj + j))],
        out_specs=pl.BlockSpec((bm, bn), lambda i, j: (i, j)),
        out_shape=jax.ShapeDtypeStruct((rows, d), BF16),
        compiler_params=_params(2),
        name="gated_merge",
    )(ro, ho, wr_stack, wh_stack, proj, proj, b_gate, b_gate)


def _ffn_up_kernel(x_ref, wg_ref, wv_ref, cwg_ref, cwv_ref, cbg_ref, cbv_ref, o_ref,
                   ug_ref, uv_ref, *, bm, tiles_per_batch):
    i = pl.program_id(1)

    @pl.when(i % tiles_per_batch == 0)
    def _():
        ug_ref[0:SUBLANES, :] = jnp.zeros((SUBLANES, ug_ref.shape[1]), F32)
        uv_ref[0:SUBLANES, :] = jnp.zeros((SUBLANES, uv_ref.shape[1]), F32)

    x = x_ref[...]
    ug_ref[SUBLANES:SUBLANES + bm, :] = jnp.dot(x, wg_ref[...], preferred_element_type=F32)
    uv_ref[SUBLANES:SUBLANES + bm, :] = jnp.dot(x, wv_ref[...], preferred_element_type=F32)

    def conv(u_ref, cw_ref, cb_ref):
        acc = cb_ref[...]
        for t in range(CONV_WIDTH):
            start = SUBLANES - (CONV_WIDTH - 1) + t
            acc = acc + cw_ref[t:t + 1, :] * u_ref[start:start + bm, :]
        return acc

    gate = conv(ug_ref, cwg_ref, cbg_ref)
    val = conv(uv_ref, cwv_ref, cbv_ref)
    o_ref[...] = (_silu(gate) * val).astype(o_ref.dtype)
    ug_ref[0:SUBLANES, :] = ug_ref[bm:bm + SUBLANES, :]
    uv_ref[0:SUBLANES, :] = uv_ref[bm:bm + SUBLANES, :]


def _ffn_up(x, w_stack, layer, conv_w, conv_b, bm, l_pad):
    rows, d = x.shape
    f2 = w_stack.shape[-1]
    f = f2 // 2
    bn = _pick_div(f, 512, MXU_COLS)
    nj = f // bn
    return pl.pallas_call(
        functools.partial(_ffn_up_kernel, bm=bm, tiles_per_batch=l_pad // bm),
        grid=(nj, rows // bm),
        in_specs=[pl.BlockSpec((bm, d), lambda j, i: (i, 0)),
                  pl.BlockSpec((None, d, bn), lambda j, i: (layer, 0, j)),
                  pl.BlockSpec((None, d, bn), lambda j, i: (layer, 0, nj + j)),
                  pl.BlockSpec((CONV_WIDTH, bn), lambda j, i: (0, j)),
                  pl.BlockSpec((CONV_WIDTH, bn), lambda j, i: (0, nj + j)),
                  pl.BlockSpec((1, bn), lambda j, i: (0, j)),
                  pl.BlockSpec((1, bn), lambda j, i: (0, nj + j))],
        out_specs=pl.BlockSpec((bm, bn), lambda j, i: (i, j)),
        out_shape=jax.ShapeDtypeStruct((rows, f), BF16),
        scratch_shapes=[pltpu.VMEM((bm + SUBLANES, bn), F32), pltpu.VMEM((bm + SUBLANES, bn), F32)],
        compiler_params=_params(2),
        name="ffn_up_conv_gate",
    )(x, w_stack, w_stack, conv_w, conv_w, conv_b, conv_b)


def _ret_kernel(lg_ref, q_ref, k_ref, v_ref, g_ref, cos_ref, sin_ref, o_ref, s_ref, *, n_chunks, heads):
    c = RET_CHUNK
    dk = RET_DK
    half = dk // 2
    scale = dk ** -0.5

    @pl.when(pl.program_id(2) == 0)
    def _():
        s_ref[...] = jnp.zeros(s_ref.shape, F32)

    ii = lax.broadcasted_iota(jnp.int32, (c, c), 0)
    jj = lax.broadcasted_iota(jnp.int32, (c, c), 1)
    diff = (ii - jj).astype(F32)
    row = lax.broadcasted_iota(jnp.int32, (c, dk), 0).astype(F32)
    decay, w_q, w_k, chunk_decay = [], [], [], []
    for g in range(heads):
        lg = lg_ref[pl.program_id(1) * heads + g]
        decay.append(jnp.where(diff >= 0, jnp.exp(jnp.maximum(diff, 0.0) * lg), 0.0))
        w_q.append(jnp.exp((row + 1.0) * lg))
        w_k.append(jnp.exp((c - 1.0 - row) * lg))
        chunk_decay.append(jnp.exp(jnp.full((1, dk), c, F32) * lg))

    def rot(x, cos, sin):
        x1, x2 = x[:, :half], x[:, half:]
        return jnp.concatenate([x1 * cos - x2 * sin, x1 * sin + x2 * cos], axis=-1)

    def body(n, carry):
        r0 = pl.multiple_of(n * c, c)
        rows = pl.ds(r0, c)
        cos = cos_ref[rows, :]
        sin = sin_ref[rows, :]
        for g in range(heads):
            cols = slice(g * dk, (g + 1) * dk)
            q = rot(q_ref[rows, cols].astype(F32), cos, sin)
            k = rot(k_ref[rows, cols].astype(F32), cos, sin) * scale
            v = v_ref[rows, cols].astype(BF16)
            state = s_ref[g]
            scores = lax.dot_general(q.astype(BF16), k.astype(BF16), NT,
                                     preferred_element_type=F32) * decay[g]
            o = jnp.dot(scores.astype(BF16), v, preferred_element_type=F32)
            o = o + jnp.dot((q * w_q[g]).astype(BF16), state.astype(BF16), preferred_element_type=F32)
            s_ref[g] = state * chunk_decay[g] + lax.dot_general(
                (k * w_k[g]).astype(BF16), v, TN, preferred_element_type=F32)
            mu = jnp.mean(o, -1, keepdims=True)
            oc = o - mu
            var = jnp.mean(oc * oc, -1, keepdims=True)
            y = oc * lax.rsqrt(var + LN_EPS) * _silu(g_ref[rows, cols].astype(F32))
            o_ref[rows, cols] = y.astype(o_ref.dtype)
        return carry

    lax.fori_loop(0, n_chunks, body, 0, unroll=_unroll(n_chunks))


def _retention(proj3, cos, sin, log_gamma, n_heads):
    b, l_pad, _ = proj3.shape
    heads = 2 if n_heads % 2 == 0 else 1
    groups = n_heads // heads
    width = heads * RET_DK
    rb = _pick_div(l_pad, SEQ_ROWS_TARGET, RET_CHUNK)

    def col(section):
        return pl.BlockSpec((None, rb, width), lambda bi, hg, r: (bi, r, section * groups + hg))

    tab = pl.BlockSpec((rb, RET_DK // 2), lambda bi, hg, r: (r, 0))
    return pl.pallas_call(
        functools.partial(_ret_kernel, n_chunks=rb // RET_CHUNK, heads=heads),
        grid=(b, groups, l_pad // rb),
        in_specs=[pl.BlockSpec(memory_space=pltpu.SMEM), col(0), col(1), col(2), col(3), tab, tab],
        out_specs=pl.BlockSpec((None, rb, width), lambda bi, hg, r: (bi, r, hg)),
        out_shape=jax.ShapeDtypeStruct((b, l_pad, n_heads * RET_DK), BF16),
        scratch_shapes=[pltpu.VMEM((heads, RET_DK, RET_DK), F32)],
        compiler_params=_params(3),
        name="retention",
    )(log_gamma, proj3, proj3, proj3, proj3, cos, sin)


def _hgrn_constants():
    t = HGRN_BLOCK
    idx = np.arange(t)
    sums, upper, pair = [], [], []
    for s in HGRN_LEVELS:
        r = (idx // (2 * s)) * 2 * s + s - 1
        up = (idx % (2 * s)) >= s
        tt = idx[None, :]
        if s < SUBLANES:
            sums.append(np.where(up[:, None], (tt > r[:, None]) & (tt <= idx[:, None]),
                                 (tt > idx[:, None]) & (tt <= r[:, None])))
        upper.append(np.broadcast_to(up[:, None], (t, HGRN_DK)))
        same = (idx[:, None] // (2 * s)) == (idx[None, :] // (2 * s))
        pair.append(same & up[:, None] & ~up[None, :])
    sums.append(idx[None, :] <= idx[:, None])
    pair.append(np.eye(t, dtype=bool))
    return (np.tile(np.concatenate(sums, 0), (1, 3)).astype(np.float32), np.stack(upper).astype(np.float32),
            np.tile(np.stack(pair), (1, 1, 2)).astype(np.float32))


def _hgrn_kernel(q_ref, f_ref, v_ref, g_ref, loglb_ref, l1mlb_ref, ng_ref, sums_ref, upper_ref,
                 pair_ref, o_ref, st_ref, *, n_blocks, heads):
    t = HGRN_BLOCK
    dk = HGRN_DK
    nl = len(HGRN_LEVELS)
    n_mxu = sum(1 for s in HGRN_LEVELS if s < SUBLANES)

    @pl.when(pl.program_id(2) == 0)
    def _():
        st_ref[...] = jnp.zeros(st_ref.shape, F32)

    log_lb = loglb_ref[...]
    log_1m_lb = l1mlb_ref[...]
    norm_g = ng_ref[...]

    def prepare(n):
        rows = pl.ds(pl.multiple_of(n * t, t), t)
        hf = f_ref[rows, :].astype(F32)
        log_sig = jnp.minimum(hf, 0.0) - jnp.log(1.0 + jnp.exp(-jnp.abs(hf)))
        b2 = log_1m_lb + log_sig
        logf = jnp.maximum(log_lb, b2) + jnp.log(1.0 + jnp.exp(-jnp.abs(log_lb - b2)))
        k_all = jnp.exp(b2 - hf)
        q_all = _silu(q_ref[rows, :].astype(F32))

        f_hi = logf.astype(BF16)
        rem = logf - f_hi.astype(F32)
        f_mid = rem.astype(BF16)
        f_lo = (rem - f_mid.astype(F32)).astype(BF16)
        e_all = jnp.dot(sums_ref[...], jnp.concatenate([f_hi, f_mid, f_lo], axis=0),
                        preferred_element_type=F32)

        z_out = [[None] * heads for _ in range(nl)]
        qk_out = [[None] * heads for _ in range(4)]
        dec_out = [None] * heads
        for g in range(heads):
            cols = slice(g * dk, (g + 1) * dk)
            q, k = q_all[:, cols], k_all[:, cols]
            prefix = e_all[n_mxu * t:(n_mxu + 1) * t, cols]
            for lvl, s in enumerate(HGRN_LEVELS):
                if lvl < n_mxu:
                    ex = e_all[lvl * t:(lvl + 1) * t, cols]
                    src = jnp.where(upper_ref[lvl] > 0.0, q, k)
                else:
                    ex_parts, src_parts = [], []
                    for r0 in range(0, t, 2 * s):
                        lo, hi = slice(r0, r0 + s), slice(r0 + s, r0 + 2 * s)
                        bound = prefix[r0 + s - 1:r0 + s, :]
                        ex_parts += [bound - prefix[lo, :], prefix[hi, :] - bound]
                        src_parts += [k[lo, :], q[hi, :]]
                    ex = jnp.concatenate(ex_parts, axis=0)
                    src = jnp.concatenate(src_parts, axis=0)
                z_out[lvl][g] = (src * jnp.exp(ex)).astype(BF16)

            total = prefix[t - 1:t, :]
            qk_out[0][g] = q.astype(BF16)
            qk_out[1][g] = k.astype(BF16)
            qk_out[2][g] = (q * jnp.exp(prefix)).astype(BF16)
            qk_out[3][g] = (k * jnp.exp(total - prefix)).astype(BF16)
            dec_out[g] = jnp.exp(total)

        def cat(parts):
            return jnp.concatenate(parts, axis=1)

        return (tuple(cat(zl) for zl in z_out), tuple(cat(x) for x in qk_out), cat(dec_out))

    zero_tile = jnp.zeros((t, dk), BF16)

    def block_diag(x):
        return jnp.concatenate([jnp.concatenate([x[:, :dk], zero_tile], axis=1),
                                jnp.concatenate([zero_tile, x[:, dk:]], axis=1)], axis=0)

    def consume(n, prepared):
        z_all, qk_all, dec_all = prepared
        rows = pl.ds(pl.multiple_of(n * t, t), t)
        for p in range(heads // 2):
            cols2 = slice(2 * p * dk, 2 * (p + 1) * dk)
            a = pair_ref[nl] * lax.dot_general(qk_all[0][:, cols2], block_diag(qk_all[1][:, cols2]), NT,
                                               preferred_element_type=F32)
            for lvl in range(nl):
                z = z_all[lvl][:, cols2]
                a = a + pair_ref[lvl] * lax.dot_general(z, block_diag(z), NT, preferred_element_type=F32)
            v = v_ref[rows, cols2].astype(BF16)
            st = [st_ref[2 * p], st_ref[2 * p + 1]]
            st_pair = jnp.concatenate(st, axis=1).astype(BF16)
            o = jnp.dot(a.astype(BF16), block_diag(v), preferred_element_type=F32)
            o = o + lax.dot_general(qk_all[2][:, cols2], block_diag(st_pair), NT, preferred_element_type=F32)
            for h in range(2):
                cols = slice((2 * p + h) * dk, (2 * p + h + 1) * dk)
                half = slice(h * dk, (h + 1) * dk)
                st_ref[2 * p + h] = st[h] * dec_all[:, cols] + lax.dot_general(
                    v[:, half], qk_all[3][:, cols], TN, preferred_element_type=F32)
                oh = o[:, half]
                y = oh * lax.rsqrt(jnp.mean(oh * oh, -1, keepdims=True) + NORM_EPS)
                gate = norm_g[:, cols] * _silu(g_ref[rows, cols].astype(F32))
                o_ref[rows, cols] = (y * gate).astype(o_ref.dtype)

    def body(n, prepared):
        consume(n, prepared)
        return prepare(jnp.minimum(n + 1, n_blocks - 1))

    lax.fori_loop(0, n_blocks, body, prepare(0), unroll=_unroll(n_blocks))


def _hgrn(proj3, col_off, log_lb, log_1m_lb, norm_g, n_heads):
    b, l_pad, _ = proj3.shape
    sums, upper, pair = _hgrn_constants()
    heads = next(h for h in (4, 2) if n_heads % h == 0)
    groups = n_heads // heads
    width = heads * HGRN_DK
    base = col_off // width
    rb = _pick_div(l_pad, SEQ_ROWS_TARGET, HGRN_BLOCK)

    def col(section):
        return pl.BlockSpec((None, rb, width), lambda bi, hg, r: (bi, r, base + section * groups + hg))

    vec = pl.BlockSpec((1, width), lambda bi, hg, r: (0, hg))

    def const(shape):
        return pl.BlockSpec(shape, lambda bi, hg, r: (0,) * len(shape))

    return pl.pallas_call(
        functools.partial(_hgrn_kernel, n_blocks=rb // HGRN_BLOCK, heads=heads),
        grid=(b, groups, l_pad // rb),
        in_specs=[col(0), col(1), col(2), col(3), vec, vec, vec,
                  const(sums.shape), const(upper.shape), const(pair.shape)],
        out_specs=pl.BlockSpec((None, rb, width), lambda bi, hg, r: (bi, r, hg)),
        out_shape=jax.ShapeDtypeStruct((b, l_pad, n_heads * HGRN_DK), BF16),
        scratch_shapes=[pltpu.VMEM((heads, HGRN_DK, HGRN_DK), F32)],
        compiler_params=_params(3),
        name="hgrn2",
    )(proj3, proj3, proj3, proj3, log_lb, log_1m_lb, norm_g,
      jnp.asarray(sums, BF16), jnp.asarray(upper), jnp.asarray(pair))


def kernel(x, meta_tokens, emb_ln_g, emb_ln_b, hgrn_lb, w_in, b_gate, w_ret_out, w_hgrn_out, w_o,
           hgrn_norm_g, ln1_g, ln1_b, w_up, conv_w, conv_b, w_down, ln2_g, ln2_b):
    batch, seq, d = x.shape
    depth = w_in.shape[0]
    n_meta = meta_tokens.shape[0]
    l_valid = n_meta + seq
    l_pad = -(-l_valid // LANES) * LANES
    rows = batch * l_pad
    ret_w = w_ret_out.shape[1]
    hgrn_w = w_hgrn_out.shape[1]
    ret_heads = ret_w // RET_DK
    hgrn_heads = hgrn_w // HGRN_DK
    in_cols = w_in.shape[-1]
    assert in_cols == 4 * ret_w + 4 * hgrn_w + 2 * d
    alpha = float((2 * depth) ** 0.25)
    bm = _pick_div(l_pad, 1056, BF16_ROWS)
    bm_down = _pick_div(l_pad, max(bm // 2, BF16_ROWS), BF16_ROWS)

    meta = jnp.broadcast_to(meta_tokens[None].astype(x.dtype), (batch, n_meta, d))
    pad = jnp.zeros((batch, l_pad - l_valid, d), x.dtype)
    h0 = jnp.concatenate([meta, x, pad], axis=1).reshape(rows, d)
    xf, xb = _layer_norm(h0, emb_ln_g, emb_ln_b)

    pos = jnp.arange(l_pad, dtype=F32)
    half = RET_DK // 2
    inv = ROPE_BASE ** (-jnp.arange(half, dtype=F32) / half)
    ang = pos[:, None] * inv[None, :]
    cos, sin = jnp.cos(ang), jnp.sin(ang)
    log_gamma = jnp.log1p(-jnp.exp2(-5.0 - jnp.arange(ret_heads, dtype=F32)))

    lb_all = jnp.cumsum(jax.nn.softmax(hgrn_lb.astype(F32), axis=0), axis=0)
    lb_all = lb_all - lb_all[0:1]
    log_lb_all = jnp.log(lb_all)
    log_1m_lb_all = jnp.log1p(-lb_all)

    w_in_b, w_ret_b, w_hgrn_b, w_o_b, w_up_b, w_down_b = (
        w.astype(BF16) for w in (w_in, w_ret_out, w_hgrn_out, w_o, w_up, w_down))

    for layer in range(depth):
        proj = _matmul(xb, w_in_b, layer, bm, BF16)
        proj3 = proj.reshape(batch, l_pad, in_cols)
        ro = _retention(proj3, cos, sin, log_gamma, ret_heads).reshape(rows, ret_w)
        ho = _hgrn(proj3, 4 * ret_w, log_lb_all[layer:layer + 1], log_1m_lb_all[layer:layer + 1],
                   hgrn_norm_g[layer:layer + 1], hgrn_heads).reshape(rows, hgrn_w)
        merged = _merge(ro, ho, w_ret_b, w_hgrn_b, layer, proj, 4 * ret_w + 4 * hgrn_w,
                        b_gate[layer:layer + 1], bm)
        z1 = _matmul_resid(merged, w_o_b, layer, xf, alpha, bm, 1)
        x1f, x1b = _layer_norm(z1, ln1_g[layer], ln1_b[layer])
        act = _ffn_up(x1b, w_up_b, layer, conv_w[layer], conv_b[layer:layer + 1], bm, l_pad)
        z2 = _matmul_resid(act, w_down_b, layer, x1f, alpha, bm_down, 1)
        xf, xb = _layer_norm(z2, ln2_g[layer], ln2_b[layer])

    return xf.reshape(batch, l_pad, d)[:, n_meta:l_valid]
```

```python
import functools
import math

import numpy as np
import jax
import jax.numpy as jnp
from jax import lax
from jax.experimental import pallas as pl
from jax.experimental.pallas import tpu as pltpu

F32 = jnp.float32
BF16 = jnp.bfloat16

N_META = 16
RET_DK = 256
RET_CHUNK = 128
HGRN_DK = 128
HGRN_BLOCK = 128
HGRN_LEVELS = (1, 2, 4, 8, 16, 32, 64)
CONV_WIDTH = 3
ROPE_BASE = 10000.0
LN_EPS = 1e-5
NORM_EPS = 1e-6

V7X_VMEM_BYTES = 64 * 1024 * 1024
VMEM_LIMIT = V7X_VMEM_BYTES - 8 * 1024 * 1024
SUBLANES = 8
LANES = 128
MXU_COLS = 256
BF16_ROWS = 16
SEQ_ROWS_TARGET = 8192

NT = (((1,), (1,)), ((), ()))
TN = (((0,), (0,)), ((), ()))


def _params(n_axes):
    return pltpu.CompilerParams(dimension_semantics=("arbitrary",) * n_axes,
                                vmem_limit_bytes=VMEM_LIMIT)


def _pick_div(n, target, mult):
    best = 0
    for d in range(mult, min(n, target) + 1, mult):
        if n % d == 0:
            best = d
    assert best > 0, (n, target, mult)
    return best


def _unroll(trips):
    return next(u for u in (3, 2, 1) if trips % u == 0)


def _sigmoid(x):
    return 0.5 + 0.5 * jnp.tanh(0.5 * x)


def _silu(x):
    h = 0.5 * x
    return h + h * jnp.tanh(h)


def _ln_kernel(x_ref, g_ref, b_ref, of_ref, ob_ref):
    x = x_ref[...]
    mu = jnp.mean(x, -1, keepdims=True)
    xc = x - mu
    var = jnp.mean(xc * xc, -1, keepdims=True)
    y = xc * lax.rsqrt(var + LN_EPS) * g_ref[...] + b_ref[...]
    of_ref[...] = y
    ob_ref[...] = y.astype(BF16)


def _layer_norm(x2d, g, b):
    rows, d = x2d.shape
    bm = _pick_div(rows, 352, BF16_ROWS)
    row_spec = pl.BlockSpec((bm, d), lambda i: (i, 0))
    vec_spec = pl.BlockSpec((1, d), lambda i: (0, 0))
    return pl.pallas_call(
        _ln_kernel,
        grid=(rows // bm,),
        in_specs=[row_spec, vec_spec, vec_spec],
        out_specs=[row_spec, row_spec],
        out_shape=[jax.ShapeDtypeStruct((rows, d), F32), jax.ShapeDtypeStruct((rows, d), BF16)],
        compiler_params=_params(1),
        name="layer_norm",
    )(x2d, g.reshape(1, d), b.reshape(1, d))


def _in_proj_kernel(a_ref, w_ref, *refs, n_cast):
    cast_in, o_ref, cast_out = refs[:n_cast], refs[n_cast], refs[n_cast + 1:]
    o_ref[...] = jnp.dot(a_ref[...], w_ref[...], preferred_element_type=F32).astype(o_ref.dtype)
    for src, dst in zip(cast_in, cast_out):
        dst[...] = src[...].astype(dst.dtype)


def _cast_rows(k, steps):
    return next(r for r in range(BF16_ROWS, k + 1, BF16_ROWS) if k % r == 0 and k // r <= steps)


def _in_proj(a, w_stack, layer, bm, casts):
    rows, k = a.shape
    n = w_stack.shape[-1]
    bn = _pick_div(n, 1024, LANES)
    nj = n // bn
    steps = (rows // bm) * nj
    cast_in_specs, cast_out_specs, cast_shapes = [], [], []
    for w, lw in casts:
        kw, nw = w.shape[1:]
        rb = _cast_rows(kw, steps)
        last = kw // rb - 1
        cast_in_specs.append(pl.BlockSpec(
            (None, rb, nw), lambda i, j, lw=lw, last=last: (lw, jnp.minimum(i * nj + j, last), 0)))
        cast_out_specs.append(pl.BlockSpec(
            (None, rb, nw), lambda i, j, last=last: (0, jnp.minimum(i * nj + j, last), 0)))
        cast_shapes.append(jax.ShapeDtypeStruct((1, kw, nw), BF16))
    outs = pl.pallas_call(
        functools.partial(_in_proj_kernel, n_cast=len(casts)),
        grid=(rows // bm, nj),
        in_specs=[pl.BlockSpec((bm, k), lambda i, j: (i, 0)),
                  pl.BlockSpec((None, k, bn), lambda i, j: (layer, 0, j))] + cast_in_specs,
        out_specs=[pl.BlockSpec((bm, bn), lambda i, j: (i, j))] + cast_out_specs,
        out_shape=[jax.ShapeDtypeStruct((rows, n), BF16)] + cast_shapes,
        compiler_params=_params(2),
        name="in_proj",
    )(a, w_stack, *[w for w, _ in casts])
    return outs[0], outs[1:]


def _mm_resid_kernel(a_ref, w_ref, r_ref, o_ref, *, alpha):
    o_ref[...] = alpha * r_ref[...] + jnp.dot(a_ref[...], w_ref[...], preferred_element_type=F32)


def _matmul_resid(a, w_stack, layer, resid, alpha, bm):
    rows, k = a.shape
    n = w_stack.shape[-1]
    bn = _pick_div(n, 512, LANES)
    return pl.pallas_call(
        functools.partial(_mm_resid_kernel, alpha=alpha),
        grid=(rows // bm, n // bn),
        in_specs=[pl.BlockSpec((bm, k), lambda i, j: (i, 0)),
                  pl.BlockSpec((None, k, bn), lambda i, j: (layer, 0, j)),
                  pl.BlockSpec((bm, bn), lambda i, j: (i, j))],
        out_specs=pl.BlockSpec((bm, bn), lambda i, j: (i, j)),
        out_shape=jax.ShapeDtypeStruct((rows, n), F32),
        compiler_params=_params(2),
        name="proj_resid",
    )(a, w_stack, resid)


def _merge_kernel(ro_ref, ho_ref, wr_ref, wh_ref, gr_ref, gh_ref, bgr_ref, bgh_ref, o_ref):
    r = jnp.dot(ro_ref[...], wr_ref[...], preferred_element_type=F32)
    h = jnp.dot(ho_ref[...], wh_ref[...], preferred_element_type=F32)
    g_r = _sigmoid(gr_ref[...].astype(F32) + bgr_ref[...])
    g_h = _sigmoid(gh_ref[...].astype(F32) + bgh_ref[...])
    o_ref[...] = (g_r * r + g_h * h).astype(o_ref.dtype)


def _merge(ro, ho, wr_stack, wh_stack, layer, proj, gate_off, b_gate, bm):
    rows, rw = ro.shape
    hw = ho.shape[1]
    d = wr_stack.shape[-1]
    bn = _pick_div(math.gcd(d, gate_off), 512, MXU_COLS)
    nj = d // bn
    goff = gate_off // bn
    return pl.pallas_call(
        _merge_kernel,
        grid=(rows // bm, nj),
        in_specs=[pl.BlockSpec((bm, rw), lambda i, j: (i, 0)),
                  pl.BlockSpec((bm, hw), lambda i, j: (i, 0)),
                  pl.BlockSpec((None, rw, bn), lambda i, j: (layer, 0, j)),
                  pl.BlockSpec((None, hw, bn), lambda i, j: (layer, 0, j)),
                  pl.BlockSpec((bm, bn), lambda i, j: (i, goff + j)),
                  pl.BlockSpec((bm, bn), lambda i, j: (i, goff + nj + j)),
                  pl.BlockSpec((1, bn), lambda i, j: (0, j)),
                  pl.BlockSpec((1, bn), lambda i, j: (0, nj + j))],
        out_specs=pl.BlockSpec((bm, bn), lambda i, j: (i, j)),
        out_shape=jax.ShapeDtypeStruct((rows, d), BF16),
        compiler_params=_params(2),
        name="gated_merge",
    )(ro, ho, wr_stack, wh_stack, proj, proj, b_gate, b_gate)


def _ffn_up_kernel(x_ref, wg_ref, wv_ref, cwg_ref, cwv_ref, cbg_ref, cbv_ref, o_ref,
                   ug_ref, uv_ref, *, bm, tiles_per_batch):
    i = pl.program_id(1)

    @pl.when(i % tiles_per_batch == 0)
    def _():
        ug_ref[0:SUBLANES, :] = jnp.zeros((SUBLANES, ug_ref.shape[1]), F32)
        uv_ref[0:SUBLANES, :] = jnp.zeros((SUBLANES, uv_ref.shape[1]), F32)

    x = x_ref[...]
    ug_ref[SUBLANES:SUBLANES + bm, :] = jnp.dot(x, wg_ref[...], preferred_element_type=F32)
    uv_ref[SUBLANES:SUBLANES + bm, :] = jnp.dot(x, wv_ref[...], preferred_element_type=F32)

    def conv(u_ref, cw_ref, cb_ref):
        acc = cb_ref[...]
        for t in range(CONV_WIDTH):
            start = SUBLANES - (CONV_WIDTH - 1) + t
            acc = acc + cw_ref[t:t + 1, :] * u_ref[start:start + bm, :]
        return acc

    gate = conv(ug_ref, cwg_ref, cbg_ref)
    val = conv(uv_ref, cwv_ref, cbv_ref)
    o_ref[...] = (_silu(gate) * val).astype(o_ref.dtype)
    ug_ref[0:SUBLANES, :] = ug_ref[bm:bm + SUBLANES, :]
    uv_ref[0:SUBLANES, :] = uv_ref[bm:bm + SUBLANES, :]


def _ffn_up(x, w_stack, layer, conv_w, conv_b, bm, l_pad):
    rows, d = x.shape
    f2 = w_stack.shape[-1]
    f = f2 // 2
    bn = _pick_div(f, 512, MXU_COLS)
    nj = f // bn
    return pl.pallas_call(
        functools.partial(_ffn_up_kernel, bm=bm, tiles_per_batch=l_pad // bm),
        grid=(nj, rows // bm),
        in_specs=[pl.BlockSpec((bm, d), lambda j, i: (i, 0)),
                  pl.BlockSpec((None, d, bn), lambda j, i: (layer, 0, j)),
                  pl.BlockSpec((None, d, bn), lambda j, i: (layer, 0, nj + j)),
                  pl.BlockSpec((CONV_WIDTH, bn), lambda j, i: (0, j)),
                  pl.BlockSpec((CONV_WIDTH, bn), lambda j, i: (0, nj + j)),
                  pl.BlockSpec((1, bn), lambda j, i: (0, j)),
                  pl.BlockSpec((1, bn), lambda j, i: (0, nj + j))],
        out_specs=pl.BlockSpec((bm, bn), lambda j, i: (i, j)),
        out_shape=jax.ShapeDtypeStruct((rows, f), BF16),
        scratch_shapes=[pltpu.VMEM((bm + SUBLANES, bn), F32), pltpu.VMEM((bm + SUBLANES, bn), F32)],
        compiler_params=_params(2),
        name="ffn_up_conv_gate",
    )(x, w_stack, w_stack, conv_w, conv_w, conv_b, conv_b)


def _ret_kernel(lg_ref, q_ref, k_ref, v_ref, g_ref, cos_ref, sin_ref, o_ref, s_ref, *, n_chunks, heads):
    c = RET_CHUNK
    dk = RET_DK
    half = dk // 2
    scale = dk ** -0.5

    @pl.when(pl.program_id(2) == 0)
    def _():
        s_ref[...] = jnp.zeros(s_ref.shape, F32)

    ii = lax.broadcasted_iota(jnp.int32, (c, c), 0)
    jj = lax.broadcasted_iota(jnp.int32, (c, c), 1)
    diff = (ii - jj).astype(F32)
    row = lax.broadcasted_iota(jnp.int32, (c, dk), 0).astype(F32)
    decay, w_q, w_k, chunk_decay = [], [], [], []
    for g in range(heads):
        lg = lg_ref[pl.program_id(1) * heads + g]
        decay.append(jnp.where(diff >= 0, jnp.exp(jnp.maximum(diff, 0.0) * lg), 0.0))
        w_q.append(jnp.exp((row + 1.0) * lg))
        w_k.append(jnp.exp((c - 1.0 - row) * lg))
        chunk_decay.append(jnp.exp(jnp.full((1, dk), c, F32) * lg))

    def rot(x, cos, sin):
        x1, x2 = x[:, :half], x[:, half:]
        return jnp.concatenate([x1 * cos - x2 * sin, x1 * sin + x2 * cos], axis=-1)

    def body(n, carry):
        r0 = pl.multiple_of(n * c, c)
        rows = pl.ds(r0, c)
        cos = cos_ref[rows, :]
        sin = sin_ref[rows, :]
        for g in range(heads):
            cols = slice(g * dk, (g + 1) * dk)
            q = rot(q_ref[rows, cols].astype(F32), cos, sin)
            k = rot(k_ref[rows, cols].astype(F32), cos, sin) * scale
            v = v_ref[rows, cols].astype(BF16)
            state = s_ref[g]
            scores = lax.dot_general(q.astype(BF16), k.astype(BF16), NT,
                                     preferred_element_type=F32) * decay[g]
            o = jnp.dot(scores.astype(BF16), v, preferred_element_type=F32)
            o = o + jnp.dot((q * w_q[g]).astype(BF16), state.astype(BF16), preferred_element_type=F32)
            s_ref[g] = state * chunk_decay[g] + lax.dot_general(
                (k * w_k[g]).astype(BF16), v, TN, preferred_element_type=F32)
            mu = jnp.mean(o, -1, keepdims=True)
            oc = o - mu
            var = jnp.mean(oc * oc, -1, keepdims=True)
            y = oc * lax.rsqrt(var + LN_EPS) * _silu(g_ref[rows, cols].astype(F32))
            o_ref[rows, cols] = y.astype(o_ref.dtype)
        return carry

    lax.fori_loop(0, n_chunks, body, 0, unroll=_unroll(n_chunks))


def _retention(proj3, cos, sin, log_gamma, n_heads):
    b, l_pad, _ = proj3.shape
    heads = 2 if n_heads % 2 == 0 else 1
    groups = n_heads // heads
    width = heads * RET_DK
    rb = _pick_div(l_pad, SEQ_ROWS_TARGET, RET_CHUNK)

    def col(section):
        return pl.BlockSpec((None, rb, width), lambda bi, hg, r: (bi, r, section * groups + hg))

    tab = pl.BlockSpec((rb, RET_DK // 2), lambda bi, hg, r: (r, 0))
    return pl.pallas_call(
        functools.partial(_ret_kernel, n_chunks=rb // RET_CHUNK, heads=heads),
        grid=(b, groups, l_pad // rb),
        in_specs=[pl.BlockSpec(memory_space=pltpu.SMEM), col(0), col(1), col(2), col(3), tab, tab],
        out_specs=pl.BlockSpec((None, rb, width), lambda bi, hg, r: (bi, r, hg)),
        out_shape=jax.ShapeDtypeStruct((b, l_pad, n_heads * RET_DK), BF16),
        scratch_shapes=[pltpu.VMEM((heads, RET_DK, RET_DK), F32)],
        compiler_params=_params(3),
        name="retention",
    )(log_gamma, proj3, proj3, proj3, proj3, cos, sin)


def _hgrn_constants():
    t = HGRN_BLOCK
    idx = np.arange(t)
    sums, upper, pair = [], [], []
    for s in HGRN_LEVELS:
        r = (idx // (2 * s)) * 2 * s + s - 1
        up = (idx % (2 * s)) >= s
        tt = idx[None, :]
        if s < SUBLANES:
            sums.append(np.where(up[:, None], (tt > r[:, None]) & (tt <= idx[:, None]),
                                 (tt > idx[:, None]) & (tt <= r[:, None])))
        upper.append(np.broadcast_to(up[:, None], (t, HGRN_DK)))
        same = (idx[:, None] // (2 * s)) == (idx[None, :] // (2 * s))
        pair.append(same & up[:, None] & ~up[None, :])
    sums.append(idx[None, :] <= idx[:, None])
    pair.append(np.eye(t, dtype=bool))
    return (np.tile(np.concatenate(sums, 0), (1, 3)).astype(np.float32), np.stack(upper).astype(np.float32),
            np.tile(np.stack(pair), (1, 1, 2)).astype(np.float32))


def _hgrn_kernel(q_ref, f_ref, v_ref, g_ref, loglb_ref, l1mlb_ref, ng_ref, sums_ref, upper_ref,
                 pair_ref, o_ref, st_ref, *, n_blocks, heads):
    t = HGRN_BLOCK
    dk = HGRN_DK
    nl = len(HGRN_LEVELS)
    n_mxu = sum(1 for s in HGRN_LEVELS if s < SUBLANES)

    @pl.when(pl.program_id(2) == 0)
    def _():
        st_ref[...] = jnp.zeros(st_ref.shape, F32)

    log_lb = loglb_ref[...]
    log_1m_lb = l1mlb_ref[...]
    norm_g = ng_ref[...]

    def prepare(n):
        rows = pl.ds(pl.multiple_of(n * t, t), t)
        hf = f_ref[rows, :].astype(F32)
        log_sig = jnp.minimum(hf, 0.0) - jnp.log(1.0 + jnp.exp(-jnp.abs(hf)))
        b2 = log_1m_lb + log_sig
        logf = jnp.maximum(log_lb, b2) + jnp.log(1.0 + jnp.exp(-jnp.abs(log_lb - b2)))
        k_all = jnp.exp(b2 - hf)
        q_all = _silu(q_ref[rows, :].astype(F32))

        f_hi = logf.astype(BF16)
        rem = logf - f_hi.astype(F32)
        f_mid = rem.astype(BF16)
        f_lo = (rem - f_mid.astype(F32)).astype(BF16)
        e_all = jnp.dot(sums_ref[...], jnp.concatenate([f_hi, f_mid, f_lo], axis=0),
                        preferred_element_type=F32)

        z_out = [[None] * heads for _ in range(nl)]
        qk_out = [[None] * heads for _ in range(4)]
        dec_out = [None] * heads
        for g in range(heads):
            cols = slice(g * dk, (g + 1) * dk)
            q, k = q_all[:, cols], k_all[:, cols]
            prefix = e_all[n_mxu * t:(n_mxu + 1) * t, cols]
            for lvl, s in enumerate(HGRN_LEVELS):
                if lvl < n_mxu:
                    ex = e_all[lvl * t:(lvl + 1) * t, cols]
                    src = jnp.where(upper_ref[lvl] > 0.0, q, k)
                else:
                    ex_parts, src_parts = [], []
                    for r0 in range(0, t, 2 * s):
                        lo, hi = slice(r0, r0 + s), slice(r0 + s, r0 + 2 * s)
                        bound = prefix[r0 + s - 1:r0 + s, :]
                        ex_parts += [bound - prefix[lo, :], prefix[hi, :] - bound]
                        src_parts += [k[lo, :], q[hi, :]]
                    ex = jnp.concatenate(ex_parts, axis=0)
                    src = jnp.concatenate(src_parts, axis=0)
                z_out[lvl][g] = (src * jnp.exp(ex)).astype(BF16)

            total = prefix[t - 1:t, :]
            qk_out[0][g] = q.astype(BF16)
            qk_out[1][g] = k.astype(BF16)
            qk_out[2][g] = (q * jnp.exp(prefix)).astype(BF16)
            qk_out[3][g] = (k * jnp.exp(total - prefix)).astype(BF16)
            dec_out[g] = jnp.exp(total)

        def cat(parts):
            return jnp.concatenate(parts, axis=1)

        return (tuple(cat(zl) for zl in z_out), tuple(cat(x) for x in qk_out), cat(dec_out))

    zero_tile = jnp.zeros((t, dk), BF16)

    def block_diag(x):
        return jnp.concatenate([jnp.concatenate([x[:, :dk], zero_tile], axis=1),
                                jnp.concatenate([zero_tile, x[:, dk:]], axis=1)], axis=0)

    def consume(n, prepared):
        z_all, qk_all, dec_all = prepared
        rows = pl.ds(pl.multiple_of(n * t, t), t)
        for p in range(heads // 2):
            cols2 = slice(2 * p * dk, 2 * (p + 1) * dk)
            a = pair_ref[nl] * lax.dot_general(qk_all[0][:, cols2], block_diag(qk_all[1][:, cols2]), NT,
                                               preferred_element_type=F32)
            for lvl in range(nl):
                z = z_all[lvl][:, cols2]
                a = a + pair_ref[lvl] * lax.dot_general(z, block_diag(z), NT, preferred_element_type=F32)
            v = v_ref[rows, cols2].astype(BF16)
            st = [st_ref[2 * p], st_ref[2 * p + 1]]
            st_pair = jnp.concatenate(st, axis=1).astype(BF16)
            o = jnp.dot(a.astype(BF16), block_diag(v), preferred_element_type=F32)
            o = o + lax.dot_general(qk_all[2][:, cols2], block_diag(st_pair), NT, preferred_element_type=F32)
            for h in range(2):
                cols = slice((2 * p + h) * dk, (2 * p + h + 1) * dk)
                half = slice(h * dk, (h + 1) * dk)
                st_ref[2 * p + h] = st[h] * dec_all[:, cols] + lax.dot_general(
                    v[:, half], qk_all[3][:, cols], TN, preferred_element_type=F32)
                oh = o[:, half]
                y = oh * lax.rsqrt(jnp.mean(oh * oh, -1, keepdims=True) + NORM_EPS)
                gate = norm_g[:, cols] * _silu(g_ref[rows, cols].astype(F32))
                o_ref[rows, cols] = (y * gate).astype(o_ref.dtype)

    def body(n, prepared):
        consume(n, prepared)
        return prepare(jnp.minimum(n + 1, n_blocks - 1))

    lax.fori_loop(0, n_blocks, body, prepare(0), unroll=_unroll(n_blocks))


def _hgrn(proj3, col_off, log_lb, log_1m_lb, norm_g, n_heads):
    b, l_pad, _ = proj3.shape
    sums, upper, pair = _hgrn_constants()
    heads = next(h for h in (4, 2) if n_heads % h == 0)
    groups = n_heads // heads
    width = heads * HGRN_DK
    base = col_off // width
    rb = _pick_div(l_pad, SEQ_ROWS_TARGET, HGRN_BLOCK)

    def col(section):
        return pl.BlockSpec((None, rb, width), lambda bi, hg, r: (bi, r, base + section * groups + hg))

    vec = pl.BlockSpec((1, width), lambda bi, hg, r: (0, hg))

    def const(shape):
        return pl.BlockSpec(shape, lambda bi, hg, r: (0,) * len(shape))

    return pl.pallas_call(
        functools.partial(_hgrn_kernel, n_blocks=rb // HGRN_BLOCK, heads=heads),
        grid=(b, groups, l_pad // rb),
        in_specs=[col(0), col(1), col(2), col(3), vec, vec, vec,
                  const(sums.shape), const(upper.shape), const(pair.shape)],
        out_specs=pl.BlockSpec((None, rb, width), lambda bi, hg, r: (bi, r, hg)),
        out_shape=jax.ShapeDtypeStruct((b, l_pad, n_heads * HGRN_DK), BF16),
        scratch_shapes=[pltpu.VMEM((heads, HGRN_DK, HGRN_DK), F32)],
        compiler_params=_params(3),
        name="hgrn2",
    )(proj3, proj3, proj3, proj3, log_lb, log_1m_lb, norm_g,
      jnp.asarray(sums, BF16), jnp.asarray(upper), jnp.asarray(pair))


def kernel(x, meta_tokens, emb_ln_g, emb_ln_b, hgrn_lb, w_in, b_gate, w_ret_out, w_hgrn_out, w_o,
           hgrn_norm_g, ln1_g, ln1_b, w_up, conv_w, conv_b, w_down, ln2_g, ln2_b):
    batch, seq, d = x.shape
    depth = w_in.shape[0]
    n_meta = meta_tokens.shape[0]
    l_valid = n_meta + seq
    l_pad = -(-l_valid // LANES) * LANES
    rows = batch * l_pad
    ret_w = w_ret_out.shape[1]
    hgrn_w = w_hgrn_out.shape[1]
    ret_heads = ret_w // RET_DK
    hgrn_heads = hgrn_w // HGRN_DK
    in_cols = w_in.shape[-1]
    assert in_cols == 4 * ret_w + 4 * hgrn_w + 2 * d
    alpha = float((2 * depth) ** 0.25)
    bm = _pick_div(l_pad, 1056, BF16_ROWS)
    bm_down = _pick_div(l_pad, max(2 * bm // 3, BF16_ROWS), BF16_ROWS)

    meta = jnp.broadcast_to(meta_tokens[None].astype(x.dtype), (batch, n_meta, d))
    pad = jnp.zeros((batch, l_pad - l_valid, d), x.dtype)
    h0 = jnp.concatenate([meta, x, pad], axis=1).reshape(rows, d)
    xf, xb = _layer_norm(h0, emb_ln_g, emb_ln_b)

    pos = jnp.arange(l_pad, dtype=F32)
    half = RET_DK // 2
    inv = ROPE_BASE ** (-jnp.arange(half, dtype=F32) / half)
    ang = pos[:, None] * inv[None, :]
    cos, sin = jnp.cos(ang), jnp.sin(ang)
    log_gamma = jnp.log1p(-jnp.exp2(-5.0 - jnp.arange(ret_heads, dtype=F32)))

    lb_all = jnp.cumsum(jax.nn.softmax(hgrn_lb.astype(F32), axis=0), axis=0)
    lb_all = lb_all - lb_all[0:1]
    log_lb_all = jnp.log(lb_all)
    log_1m_lb_all = jnp.log1p(-lb_all)

    w_in_l = w_in[0:1].astype(BF16)
    w_ret_b, w_hgrn_b, w_o_b = (w.astype(BF16) for w in (w_ret_out, w_hgrn_out, w_o))

    for layer in range(depth):
        casts = [(w_up, layer), (w_down, layer)] + ([(w_in, layer + 1)] if layer + 1 < depth else [])
        proj, cast = _in_proj(xb, w_in_l, 0, bm, casts)
        w_up_l, w_down_l = cast[0], cast[1]
        w_in_l = cast[2] if layer + 1 < depth else None
        proj3 = proj.reshape(batch, l_pad, in_cols)
        ro = _retention(proj3, cos, sin, log_gamma, ret_heads).reshape(rows, ret_w)
        ho = _hgrn(proj3, 4 * ret_w, log_lb_all[layer:layer + 1], log_1m_lb_all[layer:layer + 1],
                   hgrn_norm_g[layer:layer + 1], hgrn_heads).reshape(rows, hgrn_w)
        merged = _merge(ro, ho, w_ret_b, w_hgrn_b, layer, proj, 4 * ret_w + 4 * hgrn_w,
                        b_gate[layer:layer + 1], bm)
        z1 = _matmul_resid(merged, w_o_b, layer, xf, alpha, bm)
        x1f, x1b = _layer_norm(z1, ln1_g[layer], ln1_b[layer])
        act = _ffn_up(x1b, w_up_l, 0, conv_w[layer], conv_b[layer:layer + 1], bm, l_pad)
        z2 = _matmul_resid(act, w_down_l, 0, x1f, alpha, bm_down)
        xf, xb = _layer_norm(z2, ln2_g[layer], ln2_b[layer])

    return xf.reshape(batch, l_pad, d)[:, n_meta:l_valid]
```

```python
import functools
import math

import numpy as np
import jax
import jax.numpy as jnp
from jax import lax
from jax.experimental import pallas as pl
from jax.experimental.pallas import tpu as pltpu

F32 = jnp.float32
BF16 = jnp.bfloat16

N_META = 16
RET_DK = 256
RET_CHUNK = 128
HGRN_DK = 128
HGRN_BLOCK = 128
HGRN_LEVELS = (1, 2, 4, 8, 16, 32, 64)
CONV_WIDTH = 3
ROPE_BASE = 10000.0
LN_EPS = 1e-5
NORM_EPS = 1e-6
LOG2_E = 1.4426950408889634

V7X_VMEM_BYTES = 64 * 1024 * 1024
VMEM_LIMIT = V7X_VMEM_BYTES - 4 * 1024 * 1024
SUBLANES = 8
LANES = 128
MXU_COLS = 256
BF16_ROWS = 16
SEQ_ROWS_TARGET = 8192

NT = (((1,), (1,)), ((), ()))
TN = (((0,), (0,)), ((), ()))


def _params(n_axes):
    return pltpu.CompilerParams(dimension_semantics=("arbitrary",) * n_axes,
                                vmem_limit_bytes=VMEM_LIMIT)


def _pick_div(n, target, mult):
    best = 0
    for d in range(mult, min(n, target) + 1, mult):
        if n % d == 0:
            best = d
    assert best > 0, (n, target, mult)
    return best


def _unroll(trips, choices=(3, 2, 1)):
    return next(u for u in choices if trips % u == 0)


def _sigmoid(x):
    return 0.5 + 0.5 * jnp.tanh(0.5 * x)


def _silu(x):
    h = 0.5 * x
    return h + h * jnp.tanh(h)


def _ln_kernel(x_ref, g_ref, b_ref, of_ref, ob_ref):
    x = x_ref[...]
    mu = jnp.mean(x, -1, keepdims=True)
    xc = x - mu
    var = jnp.mean(xc * xc, -1, keepdims=True)
    y = xc * lax.rsqrt(var + LN_EPS) * g_ref[...] + b_ref[...]
    of_ref[...] = y
    ob_ref[...] = y.astype(BF16)


def _layer_norm(x2d, g, b):
    rows, d = x2d.shape
    bm = _pick_div(rows, 352, BF16_ROWS)
    row_spec = pl.BlockSpec((bm, d), lambda i: (i, 0))
    vec_spec = pl.BlockSpec((1, d), lambda i: (0, 0))
    return pl.pallas_call(
        _ln_kernel,
        grid=(rows // bm,),
        in_specs=[row_spec, vec_spec, vec_spec],
        out_specs=[row_spec, row_spec],
        out_shape=[jax.ShapeDtypeStruct((rows, d), F32), jax.ShapeDtypeStruct((rows, d), BF16)],
        compiler_params=_params(1),
        name="layer_norm",
    )(x2d, g.reshape(1, d), b.reshape(1, d))


def _in_proj_kernel(a_ref, w_ref, *refs, n_cast):
    cast_in, o_ref, cast_out = refs[:n_cast], refs[n_cast], refs[n_cast + 1:]
    o_ref[...] = jnp.dot(a_ref[...], w_ref[...], preferred_element_type=F32).astype(o_ref.dtype)
    for src, dst in zip(cast_in, cast_out):
        dst[...] = src[...].astype(dst.dtype)


def _cast_rows(k, steps):
    return next(r for r in range(BF16_ROWS, k + 1, BF16_ROWS) if k % r == 0 and k // r <= steps)


def _in_proj(a, w_stack, layer, bm, casts):
    rows, k = a.shape
    n = w_stack.shape[-1]
    bn = _pick_div(n, 1024, LANES)
    nj = n // bn
    steps = (rows // bm) * nj
    cast_in_specs, cast_out_specs, cast_shapes = [], [], []
    for w, lw in casts:
        kw, nw = w.shape[1:]
        rb = _cast_rows(kw, steps)
        last = kw // rb - 1
        cast_in_specs.append(pl.BlockSpec(
            (None, rb, nw), lambda i, j, lw=lw, last=last: (lw, jnp.minimum(i * nj + j, last), 0)))
        cast_out_specs.append(pl.BlockSpec(
            (None, rb, nw), lambda i, j, last=last: (0, jnp.minimum(i * nj + j, last), 0)))
        cast_shapes.append(jax.ShapeDtypeStruct((1, kw, nw), BF16))
    outs = pl.pallas_call(
        functools.partial(_in_proj_kernel, n_cast=len(casts)),
        grid=(rows // bm, nj),
        in_specs=[pl.BlockSpec((bm, k), lambda i, j: (i, 0)),
                  pl.BlockSpec((None, k, bn), lambda i, j: (layer, 0, j))] + cast_in_specs,
        out_specs=[pl.BlockSpec((bm, bn), lambda i, j: (i, j))] + cast_out_specs,
        out_shape=[jax.ShapeDtypeStruct((rows, n), BF16)] + cast_shapes,
        compiler_params=_params(2),
        name="in_proj",
    )(a, w_stack, *[w for w, _ in casts])
    return outs[0], outs[1:]


def _mm_resid_kernel(a_ref, w_ref, r_ref, o_ref, *, alpha):
    o_ref[...] = alpha * r_ref[...] + jnp.dot(a_ref[...], w_ref[...], preferred_element_type=F32)


def _matmul_resid(a, w_stack, layer, resid, alpha, bm):
    rows, k = a.shape
    n = w_stack.shape[-1]
    bn = _pick_div(n, 512, LANES)
    return pl.pallas_call(
        functools.partial(_mm_resid_kernel, alpha=alpha),
        grid=(rows // bm, n // bn),
        in_specs=[pl.BlockSpec((bm, k), lambda i, j: (i, 0)),
                  pl.BlockSpec((None, k, bn), lambda i, j: (layer, 0, j)),
                  pl.BlockSpec((bm, bn), lambda i, j: (i, j))],
        out_specs=pl.BlockSpec((bm, bn), lambda i, j: (i, j)),
        out_shape=jax.ShapeDtypeStruct((rows, n), F32),
        compiler_params=_params(2),
        name="proj_resid",
    )(a, w_stack, resid)


def _merge_kernel(ro_ref, ho_ref, wr_ref, wh_ref, gr_ref, gh_ref, bgr_ref, bgh_ref, o_ref):
    r = jnp.dot(ro_ref[...], wr_ref[...], preferred_element_type=F32)
    h = jnp.dot(ho_ref[...], wh_ref[...], preferred_element_type=F32)
    g_r = _sigmoid(gr_ref[...].astype(F32) + bgr_ref[...])
    g_h = _sigmoid(gh_ref[...].astype(F32) + bgh_ref[...])
    o_ref[...] = (g_r * r + g_h * h).astype(o_ref.dtype)


def _merge(ro, ho, wr_stack, wh_stack, layer, proj, gate_off, b_gate, bm):
    rows, rw = ro.shape
    hw = ho.shape[1]
    d = wr_stack.shape[-1]
    bn = _pick_div(math.gcd(d, gate_off), 512, MXU_COLS)
    nj = d // bn
    goff = gate_off // bn
    return pl.pallas_call(
        _merge_kernel,
        grid=(rows // bm, nj),
        in_specs=[pl.BlockSpec((bm, rw), lambda i, j: (i, 0)),
                  pl.BlockSpec((bm, hw), lambda i, j: (i, 0)),
                  pl.BlockSpec((None, rw, bn), lambda i, j: (layer, 0, j)),
                  pl.BlockSpec((None, hw, bn), lambda i, j: (layer, 0, j)),
                  pl.BlockSpec((bm, bn), lambda i, j: (i, goff + j)),
                  pl.BlockSpec((bm, bn), lambda i, j: (i, goff + nj + j)),
                  pl.BlockSpec((1, bn), lambda i, j: (0, j)),
                  pl.BlockSpec((1, bn), lambda i, j: (0, nj + j))],
        out_specs=pl.BlockSpec((bm, bn), lambda i, j: (i, j)),
        out_shape=jax.ShapeDtypeStruct((rows, d), BF16),
        compiler_params=_params(2),
        name="gated_merge",
    )(ro, ho, wr_stack, wh_stack, proj, proj, b_gate, b_gate)


def _ffn_up_kernel(x_ref, wg_ref, wv_ref, cwg_ref, cwv_ref, cbg_ref, cbv_ref, o_ref,
                   ug_ref, uv_ref, *, bm, tiles_per_batch):
    i = pl.program_id(1)

    @pl.when(i % tiles_per_batch == 0)
    def _():
        ug_ref[0:SUBLANES, :] = jnp.zeros((SUBLANES, ug_ref.shape[1]), F32)
        uv_ref[0:SUBLANES, :] = jnp.zeros((SUBLANES, uv_ref.shape[1]), F32)

    x = x_ref[...]
    ug_ref[SUBLANES:SUBLANES + bm, :] = jnp.dot(x, wg_ref[...], preferred_element_type=F32)
    uv_ref[SUBLANES:SUBLANES + bm, :] = jnp.dot(x, wv_ref[...], preferred_element_type=F32)

    def conv(u_ref, cw_ref, cb_ref):
        acc = cb_ref[...]
        for t in range(CONV_WIDTH):
            start = SUBLANES - (CONV_WIDTH - 1) + t
            acc = acc + cw_ref[t:t + 1, :] * u_ref[start:start + bm, :]
        return acc

    gate = conv(ug_ref, cwg_ref, cbg_ref)
    val = conv(uv_ref, cwv_ref, cbv_ref)
    o_ref[...] = (_silu(gate) * val).astype(o_ref.dtype)
    ug_ref[0:SUBLANES, :] = ug_ref[bm:bm + SUBLANES, :]
    uv_ref[0:SUBLANES, :] = uv_ref[bm:bm + SUBLANES, :]


def _ffn_up(x, w_stack, layer, conv_w, conv_b, bm, l_pad):
    rows, d = x.shape
    f2 = w_stack.shape[-1]
    f = f2 // 2
    bn = _pick_div(f, 512, MXU_COLS)
    nj = f // bn
    return pl.pallas_call(
        functools.partial(_ffn_up_kernel, bm=bm, tiles_per_batch=l_pad // bm),
        grid=(nj, rows // bm),
        in_specs=[pl.BlockSpec((bm, d), lambda j, i: (i, 0)),
                  pl.BlockSpec((None, d, bn), lambda j, i: (layer, 0, j)),
                  pl.BlockSpec((None, d, bn), lambda j, i: (layer, 0, nj + j)),
                  pl.BlockSpec((CONV_WIDTH, bn), lambda j, i: (0, j)),
                  pl.BlockSpec((CONV_WIDTH, bn), lambda j, i: (0, nj + j)),
                  pl.BlockSpec((1, bn), lambda j, i: (0, j)),
                  pl.BlockSpec((1, bn), lambda j, i: (0, nj + j))],
        out_specs=pl.BlockSpec((bm, bn), lambda j, i: (i, j)),
        out_shape=jax.ShapeDtypeStruct((rows, f), BF16),
        scratch_shapes=[pltpu.VMEM((bm + SUBLANES, bn), F32), pltpu.VMEM((bm + SUBLANES, bn), F32)],
        compiler_params=_params(2),
        name="ffn_up_conv_gate",
    )(x, w_stack, w_stack, conv_w, conv_w, conv_b, conv_b)


def _ret_kernel(lg_ref, q_ref, k_ref, v_ref, g_ref, cos_ref, sin_ref, o_ref, s_ref, *, n_chunks, heads):
    c = RET_CHUNK
    dk = RET_DK
    half = dk // 2
    scale = dk ** -0.5
    assert math.frexp(scale)[0] == 0.5, "key scale must be a power of two"

    @pl.when(pl.program_id(2) == 0)
    def _():
        s_ref[...] = jnp.zeros(s_ref.shape, F32)

    ii = lax.broadcasted_iota(jnp.int32, (c, c), 0)
    jj = lax.broadcasted_iota(jnp.int32, (c, c), 1)
    diff = (ii - jj).astype(F32)
    row = lax.broadcasted_iota(jnp.int32, (c, dk), 0).astype(F32)
    decay, w_q, w_k, chunk_decay = [], [], [], []
    for g in range(heads):
        lg = lg_ref[pl.program_id(1) * heads + g]
        decay.append(jnp.where(diff >= 0, jnp.exp(jnp.maximum(diff, 0.0) * lg) * scale, 0.0))
        w_q.append(jnp.exp((row + 1.0) * lg))
        w_k.append(jnp.exp((c - 1.0 - row) * lg) * scale)
        chunk_decay.append(jnp.exp(jnp.full((1, dk), c, F32) * lg))

    def rot(x, cos, sin):
        x1, x2 = x[:, :half], x[:, half:]
        return jnp.concatenate([x1 * cos - x2 * sin, x1 * sin + x2 * cos], axis=-1)

    def body(n, carry):
        r0 = pl.multiple_of(n * c, c)
        rows = pl.ds(r0, c)
        cos = cos_ref[rows, :]
        sin = sin_ref[rows, :]
        for g in range(heads):
            cols = slice(g * dk, (g + 1) * dk)
            q = rot(q_ref[rows, cols].astype(F32), cos, sin)
            k = rot(k_ref[rows, cols].astype(F32), cos, sin)
            v = v_ref[rows, cols].astype(BF16)
            state = s_ref[g]
            scores = lax.dot_general(q.astype(BF16), k.astype(BF16), NT,
                                     preferred_element_type=F32) * decay[g]
            o = jnp.dot(scores.astype(BF16), v, preferred_element_type=F32)
            o = o + jnp.dot((q * w_q[g]).astype(BF16), state.astype(BF16), preferred_element_type=F32)
            s_ref[g] = state * chunk_decay[g] + lax.dot_general(
                (k * w_k[g]).astype(BF16), v, TN, preferred_element_type=F32)
            mu = jnp.mean(o, -1, keepdims=True)
            oc = o - mu
            var = jnp.mean(oc * oc, -1, keepdims=True)
            y = oc * lax.rsqrt(var + LN_EPS) * _silu(g_ref[rows, cols].astype(F32))
            o_ref[rows, cols] = y.astype(o_ref.dtype)
        return carry

    lax.fori_loop(0, n_chunks, body, 0, unroll=_unroll(n_chunks, (11, 3, 2, 1)))


def _retention(proj3, cos, sin, log_gamma, n_heads):
    b, l_pad, _ = proj3.shape
    heads = 2 if n_heads % 2 == 0 else 1
    groups = n_heads // heads
    width = heads * RET_DK
    rb = _pick_div(l_pad, SEQ_ROWS_TARGET, RET_CHUNK)

    def col(section):
        return pl.BlockSpec((None, rb, width), lambda bi, hg, r: (bi, r, section * groups + hg))

    tab = pl.BlockSpec((rb, RET_DK // 2), lambda bi, hg, r: (r, 0))
    return pl.pallas_call(
        functools.partial(_ret_kernel, n_chunks=rb // RET_CHUNK, heads=heads),
        grid=(b, groups, l_pad // rb),
        in_specs=[pl.BlockSpec(memory_space=pltpu.SMEM), col(0), col(1), col(2), col(3), tab, tab],
        out_specs=pl.BlockSpec((None, rb, width), lambda bi, hg, r: (bi, r, hg)),
        out_shape=jax.ShapeDtypeStruct((b, l_pad, n_heads * RET_DK), BF16),
        scratch_shapes=[pltpu.VMEM((heads, RET_DK, RET_DK), F32)],
        compiler_params=_params(3),
        name="retention",
    )(log_gamma, proj3, proj3, proj3, proj3, cos, sin)


def _hgrn_constants():
    t = HGRN_BLOCK
    idx = np.arange(t)
    sums, upper, pair = [], [], []
    for s in HGRN_LEVELS:
        r = (idx // (2 * s)) * 2 * s + s - 1
        up = (idx % (2 * s)) >= s
        tt = idx[None, :]
        if s < SUBLANES:
            sums.append(np.where(up[:, None], (tt > r[:, None]) & (tt <= idx[:, None]),
                                 (tt > idx[:, None]) & (tt <= r[:, None])))
        upper.append(np.broadcast_to(up[:, None], (t, HGRN_DK)))
        same = (idx[:, None] // (2 * s)) == (idx[None, :] // (2 * s))
        pair.append(same & up[:, None] & ~up[None, :])
    sums.append(idx[None, :] <= idx[:, None])
    pair.append(np.eye(t, dtype=bool))
    return (np.tile(np.concatenate(sums, 0), (1, 3)).astype(np.float32), np.stack(upper).astype(np.float32),
            np.tile(np.stack(pair), (1, 1, 2)).astype(np.float32))


def _hgrn_kernel(q_ref, f_ref, v_ref, g_ref, loglb_ref, l1mlb_ref, ng_ref, sums_ref, upper_ref,
                 pair_ref, o_ref, st_ref, *, n_blocks, heads):
    t = HGRN_BLOCK
    dk = HGRN_DK
    nl = len(HGRN_LEVELS)
    n_mxu = sum(1 for s in HGRN_LEVELS if s < SUBLANES)

    @pl.when(pl.program_id(2) == 0)
    def _():
        st_ref[...] = jnp.zeros(st_ref.shape, F32)

    log_lb = loglb_ref[...]
    log_1m_lb = l1mlb_ref[...]
    norm_g = ng_ref[...]

    def prepare(n):
        rows = pl.ds(pl.multiple_of(n * t, t), t)
        hf = f_ref[rows, :].astype(F32)
        log_sig = jnp.minimum(hf, 0.0) - jnp.log(1.0 + jnp.exp(-jnp.abs(hf)))
        b2 = log_1m_lb + log_sig
        logf = jnp.maximum(log_lb, b2) + jnp.log(1.0 + jnp.exp(-jnp.abs(log_lb - b2)))
        k_all = jnp.exp(b2 - hf)
        q_all = _silu(q_ref[rows, :].astype(F32))

        logf = logf * LOG2_E
        f_hi = logf.astype(BF16)
        rem = logf - f_hi.astype(F32)
        f_mid = rem.astype(BF16)
        f_lo = (rem - f_mid.astype(F32)).astype(BF16)
        e_all = jnp.dot(sums_ref[...], jnp.concatenate([f_hi, f_mid, f_lo], axis=0),
                        preferred_element_type=F32)

        z_out = [[None] * heads for _ in range(nl)]
        qk_out = [[None] * heads for _ in range(4)]
        dec_out = [None] * heads
        for g in range(heads):
            cols = slice(g * dk, (g + 1) * dk)
            q, k = q_all[:, cols], k_all[:, cols]
            prefix = e_all[n_mxu * t:(n_mxu + 1) * t, cols]
            for lvl, s in enumerate(HGRN_LEVELS):
                if lvl < n_mxu:
                    ex = e_all[lvl * t:(lvl + 1) * t, cols]
                    src = jnp.where(upper_ref[lvl] > 0.0, q, k)
                else:
                    ex_parts, src_parts = [], []
                    for r0 in range(0, t, 2 * s):
                        lo, hi = slice(r0, r0 + s), slice(r0 + s, r0 + 2 * s)
                        bound = prefix[r0 + s - 1:r0 + s, :]
                        ex_parts += [bound - prefix[lo, :], prefix[hi, :] - bound]
                        src_parts += [k[lo, :], q[hi, :]]
                    ex = jnp.concatenate(ex_parts, axis=0)
                    src = jnp.concatenate(src_parts, axis=0)
                z_out[lvl][g] = (src * jnp.exp2(ex)).astype(BF16)

            total = prefix[t - 1:t, :]
            qk_out[0][g] = q.astype(BF16)
            qk_out[1][g] = k.astype(BF16)
            qk_out[2][g] = (q * jnp.exp2(prefix)).astype(BF16)
            qk_out[3][g] = (k * jnp.exp2(total - prefix)).astype(BF16)
            dec_out[g] = jnp.exp2(total)

        def cat(parts):
            return jnp.concatenate(parts, axis=1)

        return (tuple(cat(zl) for zl in z_out), tuple(cat(x) for x in qk_out), cat(dec_out))

    zero_tile = jnp.zeros((t, dk), BF16)

    def block_diag(x):
        return jnp.concatenate([jnp.concatenate([x[:, :dk], zero_tile], axis=1),
                                jnp.concatenate([zero_tile, x[:, dk:]], axis=1)], axis=0)

    def consume(n, prepared):
        z_all, qk_all, dec_all = prepared
        rows = pl.ds(pl.multiple_of(n * t, t), t)
        for p in range(heads // 2):
            cols2 = slice(2 * p * dk, 2 * (p + 1) * dk)
            a = pair_ref[nl] * lax.dot_general(qk_all[0][:, cols2], block_diag(qk_all[1][:, cols2]), NT,
                                               preferred_element_type=F32).astype(BF16)
            for lvl in range(nl):
                z = z_all[lvl][:, cols2]
                a = a + pair_ref[lvl] * lax.dot_general(z, block_diag(z), NT,
                                                        preferred_element_type=F32).astype(BF16)
            v = v_ref[rows, cols2].astype(BF16)
            st = [st_ref[2 * p], st_ref[2 * p + 1]]
            st_pair = jnp.concatenate(st, axis=1).astype(BF16)
            o = jnp.dot(a, block_diag(v), preferred_element_type=F32)
            o = o + lax.dot_general(qk_all[2][:, cols2], block_diag(st_pair), NT, preferred_element_type=F32)
            for h in range(2):
                cols = slice((2 * p + h) * dk, (2 * p + h + 1) * dk)
                half = slice(h * dk, (h + 1) * dk)
                st_ref[2 * p + h] = st[h] * dec_all[:, cols] + lax.dot_general(
                    v[:, half], qk_all[3][:, cols], TN, preferred_element_type=F32)
                oh = o[:, half]
                y = oh * lax.rsqrt(jnp.mean(oh * oh, -1, keepdims=True) + NORM_EPS)
                gate = norm_g[:, cols] * _silu(g_ref[rows, cols].astype(F32))
                o_ref[rows, cols] = (y * gate).astype(o_ref.dtype)

    def body(n, prepared):
        consume(n, prepared)
        return prepare(jnp.minimum(n + 1, n_blocks - 1))

    lax.fori_loop(0, n_blocks, body, prepare(0), unroll=_unroll(n_blocks, (11, 3, 2, 1)))


def _hgrn(proj3, col_off, log_lb, log_1m_lb, norm_g, n_heads):
    b, l_pad, _ = proj3.shape
    sums, upper, pair = _hgrn_constants()
    heads = next(h for h in (4, 2) if n_heads % h == 0)
    groups = n_heads // heads
    width = heads * HGRN_DK
    base = col_off // width
    rb = _pick_div(l_pad, SEQ_ROWS_TARGET, HGRN_BLOCK)

    def col(section):
        return pl.BlockSpec((None, rb, width), lambda bi, hg, r: (bi, r, base + section * groups + hg))

    vec = pl.BlockSpec((1, width), lambda bi, hg, r: (0, hg))

    def const(shape):
        return pl.BlockSpec(shape, lambda bi, hg, r: (0,) * len(shape))

    return pl.pallas_call(
        functools.partial(_hgrn_kernel, n_blocks=rb // HGRN_BLOCK, heads=heads),
        grid=(b, groups, l_pad // rb),
        in_specs=[col(0), col(1), col(2), col(3), vec, vec, vec,
                  const(sums.shape), const(upper.shape), const(pair.shape)],
        out_specs=pl.BlockSpec((None, rb, width), lambda bi, hg, r: (bi, r, hg)),
        out_shape=jax.ShapeDtypeStruct((b, l_pad, n_heads * HGRN_DK), BF16),
        scratch_shapes=[pltpu.VMEM((heads, HGRN_DK, HGRN_DK), F32)],
        compiler_params=_params(3),
        name="hgrn2",
    )(proj3, proj3, proj3, proj3, log_lb, log_1m_lb, norm_g,
      jnp.asarray(sums, BF16), jnp.asarray(upper), jnp.asarray(pair, BF16))


def kernel(x, meta_tokens, emb_ln_g, emb_ln_b, hgrn_lb, w_in, b_gate, w_ret_out, w_hgrn_out, w_o,
           hgrn_norm_g, ln1_g, ln1_b, w_up, conv_w, conv_b, w_down, ln2_g, ln2_b):
    batch, seq, d = x.shape
    depth = w_in.shape[0]
    n_meta = meta_tokens.shape[0]
    l_valid = n_meta + seq
    l_pad = -(-l_valid // LANES) * LANES
    rows = batch * l_pad
    ret_w = w_ret_out.shape[1]
    hgrn_w = w_hgrn_out.shape[1]
    ret_heads = ret_w // RET_DK
    hgrn_heads = hgrn_w // HGRN_DK
    in_cols = w_in.shape[-1]
    assert in_cols == 4 * ret_w + 4 * hgrn_w + 2 * d
    alpha = float((2 * depth) ** 0.25)
    bm = _pick_div(l_pad, 1056, BF16_ROWS)
    bm_down = _pick_div(l_pad, max(2 * bm // 3, BF16_ROWS), BF16_ROWS)

    meta = jnp.broadcast_to(meta_tokens[None].astype(x.dtype), (batch, n_meta, d))
    pad = jnp.zeros((batch, l_pad - l_valid, d), x.dtype)
    h0 = jnp.concatenate([meta, x, pad], axis=1).reshape(rows, d)
    xf, xb = _layer_norm(h0, emb_ln_g, emb_ln_b)

    pos = jnp.arange(l_pad, dtype=F32)
    half = RET_DK // 2
    inv = ROPE_BASE ** (-jnp.arange(half, dtype=F32) / half)
    ang = pos[:, None] * inv[None, :]
    cos, sin = jnp.cos(ang), jnp.sin(ang)
    log_gamma = jnp.log1p(-jnp.exp2(-5.0 - jnp.arange(ret_heads, dtype=F32)))

    lb_all = jnp.cumsum(jax.nn.softmax(hgrn_lb.astype(F32), axis=0), axis=0)
    lb_all = lb_all - lb_all[0:1]
    log_lb_all = jnp.log(lb_all)
    log_1m_lb_all = jnp.log1p(-lb_all)

    w_in_l = w_in[0:1].astype(BF16)

    for layer in range(depth):
        casts = [(w, layer) for w in (w_up, w_down, w_ret_out, w_hgrn_out, w_o)]
        casts += [(w_in, layer + 1)] if layer + 1 < depth else []
        proj, cast = _in_proj(xb, w_in_l, 0, bm, casts)
        w_up_l, w_down_l, w_ret_l, w_hgrn_l, w_o_l = cast[:5]
        w_in_l = cast[5] if layer + 1 < depth else None
        proj3 = proj.reshape(batch, l_pad, in_cols)
        ro = _retention(proj3, cos, sin, log_gamma, ret_heads).reshape(rows, ret_w)
        ho = _hgrn(proj3, 4 * ret_w, log_lb_all[layer:layer + 1], log_1m_lb_all[layer:layer + 1],
                   hgrn_norm_g[layer:layer + 1], hgrn_heads).reshape(rows, hgrn_w)
        merged = _merge(ro, ho, w_ret_l, w_hgrn_l, 0, proj, 4 * ret_w + 4 * hgrn_w,
                        b_gate[layer:layer + 1], bm)
        z1 = _matmul_resid(merged, w_o_l, 0, xf, alpha, bm)
        x1f, x1b = _layer_norm(z1, ln1_g[layer], ln1_b[layer])
        act = _ffn_up(x1b, w_up_l, 0, conv_w[layer], conv_b[layer:layer + 1], bm, l_pad)
        z2 = _matmul_resid(act, w_down_l, 0, x1f, alpha, bm_down)
        xf, xb = _layer_norm(z2, ln2_g[layer], ln2_b[layer])

    return xf.reshape(batch, l_pad, d)[:, n_meta:l_valid]
```

```python
import functools
import math

import numpy as np
import jax
import jax.numpy as jnp
from jax import lax
from jax.experimental import pallas as pl
from jax.experimental.pallas import tpu as pltpu

F32 = jnp.float32
BF16 = jnp.bfloat16

N_META = 16
RET_DK = 256
RET_CHUNK = 128
HGRN_DK = 128
HGRN_BLOCK = 128
HGRN_LEVELS = (1, 2, 4, 8, 16, 32, 64)
CONV_WIDTH = 3
ROPE_BASE = 10000.0
LN_EPS = 1e-5
NORM_EPS = 1e-6
LOG2_E = 1.4426950408889634

V7X_VMEM_BYTES = 64 * 1024 * 1024
VMEM_LIMIT = V7X_VMEM_BYTES - 4 * 1024 * 1024
SUBLANES = 8
LANES = 128
MXU_COLS = 256
BF16_ROWS = 16
SEQ_ROWS_TARGET = 8192

NT = (((1,), (1,)), ((), ()))
TN = (((0,), (0,)), ((), ()))


def _params(n_axes):
    return pltpu.CompilerParams(dimension_semantics=("arbitrary",) * n_axes,
                                vmem_limit_bytes=VMEM_LIMIT)


def _pick_div(n, target, mult):
    best = 0
    for d in range(mult, min(n, target) + 1, mult):
        if n % d == 0:
            best = d
    assert best > 0, (n, target, mult)
    return best


def _widest_bn(n, vmem_bytes):
    return next(c for c in (1024, 512, MXU_COLS) if n % c == 0 and vmem_bytes(c) <= VMEM_LIMIT)


def _unroll(trips, choices=(3, 2, 1)):
    return next(u for u in choices if trips % u == 0)


def _sigmoid(x):
    return 0.5 + 0.5 * jnp.tanh(0.5 * x)


def _silu(x):
    h = 0.5 * x
    return h + h * jnp.tanh(h)


def _ln_kernel(x_ref, g_ref, b_ref, of_ref, ob_ref):
    x = x_ref[...]
    mu = jnp.mean(x, -1, keepdims=True)
    xc = x - mu
    var = jnp.mean(xc * xc, -1, keepdims=True)
    y = xc * lax.rsqrt(var + LN_EPS) * g_ref[...] + b_ref[...]
    of_ref[...] = y
    ob_ref[...] = y.astype(BF16)


def _layer_norm(x2d, g, b):
    rows, d = x2d.shape
    bm = _pick_div(rows, 352, BF16_ROWS)
    row_spec = pl.BlockSpec((bm, d), lambda i: (i, 0))
    vec_spec = pl.BlockSpec((1, d), lambda i: (0, 0))
    return pl.pallas_call(
        _ln_kernel,
        grid=(rows // bm,),
        in_specs=[row_spec, vec_spec, vec_spec],
        out_specs=[row_spec, row_spec],
        out_shape=[jax.ShapeDtypeStruct((rows, d), F32), jax.ShapeDtypeStruct((rows, d), BF16)],
        compiler_params=_params(1),
        name="layer_norm",
    )(x2d, g.reshape(1, d), b.reshape(1, d))


def _in_proj_kernel(a_ref, w_ref, *refs, n_cast):
    cast_in, o_ref, cast_out = refs[:n_cast], refs[n_cast], refs[n_cast + 1:]
    o_ref[...] = jnp.dot(a_ref[...], w_ref[...], preferred_element_type=F32).astype(o_ref.dtype)
    for src, dst in zip(cast_in, cast_out):
        dst[...] = src[...].astype(dst.dtype)


def _cast_rows(k, steps):
    return next(r for r in range(BF16_ROWS, k + 1, BF16_ROWS) if k % r == 0 and k // r <= steps)


def _in_proj(a, w_stack, layer, bm, casts):
    rows, k = a.shape
    n = w_stack.shape[-1]
    bn = _pick_div(n, 1024, LANES)
    nj = n // bn
    steps = (rows // bm) * nj
    cast_in_specs, cast_out_specs, cast_shapes = [], [], []
    for w, lw in casts:
        kw, nw = w.shape[1:]
        rb = _cast_rows(kw, steps)
        last = kw // rb - 1
        cast_in_specs.append(pl.BlockSpec(
            (None, rb, nw), lambda i, j, lw=lw, last=last: (lw, jnp.minimum(i * nj + j, last), 0)))
        cast_out_specs.append(pl.BlockSpec(
            (None, rb, nw), lambda i, j, last=last: (0, jnp.minimum(i * nj + j, last), 0)))
        cast_shapes.append(jax.ShapeDtypeStruct((1, kw, nw), BF16))
    outs = pl.pallas_call(
        functools.partial(_in_proj_kernel, n_cast=len(casts)),
        grid=(rows // bm, nj),
        in_specs=[pl.BlockSpec((bm, k), lambda i, j: (i, 0)),
                  pl.BlockSpec((None, k, bn), lambda i, j: (layer, 0, j))] + cast_in_specs,
        out_specs=[pl.BlockSpec((bm, bn), lambda i, j: (i, j))] + cast_out_specs,
        out_shape=[jax.ShapeDtypeStruct((rows, n), BF16)] + cast_shapes,
        compiler_params=_params(2),
        name="in_proj",
    )(a, w_stack, *[w for w, _ in casts])
    return outs[0], outs[1:]


def _mm_resid_kernel(a_ref, w_ref, r_ref, o_ref, *, alpha):
    o_ref[...] = alpha * r_ref[...] + jnp.dot(a_ref[...], w_ref[...], preferred_element_type=F32)


def _matmul_resid(a, w_stack, layer, resid, alpha, bm):
    rows, k = a.shape
    n = w_stack.shape[-1]
    bn = _widest_bn(n, lambda c: 4 * bm * k + 4 * k * c + 20 * bm * c)
    return pl.pallas_call(
        functools.partial(_mm_resid_kernel, alpha=alpha),
        grid=(rows // bm, n // bn),
        in_specs=[pl.BlockSpec((bm, k), lambda i, j: (i, 0)),
                  pl.BlockSpec((None, k, bn), lambda i, j: (layer, 0, j)),
                  pl.BlockSpec((bm, bn), lambda i, j: (i, j))],
        out_specs=pl.BlockSpec((bm, bn), lambda i, j: (i, j)),
        out_shape=jax.ShapeDtypeStruct((rows, n), F32),
        compiler_params=_params(2),
        name="proj_resid",
    )(a, w_stack, resid)


def _merge_kernel(ro_ref, ho_ref, wr_ref, wh_ref, gr_ref, gh_ref, bgr_ref, bgh_ref, o_ref):
    r = jnp.dot(ro_ref[...], wr_ref[...], preferred_element_type=F32)
    h = jnp.dot(ho_ref[...], wh_ref[...], preferred_element_type=F32)
    g_r = _sigmoid(gr_ref[...].astype(F32) + bgr_ref[...])
    g_h = _sigmoid(gh_ref[...].astype(F32) + bgh_ref[...])
    o_ref[...] = (g_r * r + g_h * h).astype(o_ref.dtype)


def _merge(ro, ho, wr_stack, wh_stack, layer, proj, gate_off, b_gate, bm):
    rows, rw = ro.shape
    hw = ho.shape[1]
    d = wr_stack.shape[-1]
    bn = _widest_bn(math.gcd(d, gate_off), lambda c: 4 * bm * (rw + hw) + 4 * (rw + hw) * c + 20 * bm * c)
    nj = d // bn
    goff = gate_off // bn
    return pl.pallas_call(
        _merge_kernel,
        grid=(rows // bm, nj),
        in_specs=[pl.BlockSpec((bm, rw), lambda i, j: (i, 0)),
                  pl.BlockSpec((bm, hw), lambda i, j: (i, 0)),
                  pl.BlockSpec((None, rw, bn), lambda i, j: (layer, 0, j)),
                  pl.BlockSpec((None, hw, bn), lambda i, j: (layer, 0, j)),
                  pl.BlockSpec((bm, bn), lambda i, j: (i, goff + j)),
                  pl.BlockSpec((bm, bn), lambda i, j: (i, goff + nj + j)),
                  pl.BlockSpec((1, bn), lambda i, j: (0, j)),
                  pl.BlockSpec((1, bn), lambda i, j: (0, nj + j))],
        out_specs=pl.BlockSpec((bm, bn), lambda i, j: (i, j)),
        out_shape=jax.ShapeDtypeStruct((rows, d), BF16),
        compiler_params=_params(2),
        name="gated_merge",
    )(ro, ho, wr_stack, wh_stack, proj, proj, b_gate, b_gate)


def _ffn_up_kernel(x_ref, wg_ref, wv_ref, cwg_ref, cwv_ref, cbg_ref, cbv_ref, o_ref,
                   ug_ref, uv_ref, *, bm, tiles_per_batch):
    i = pl.program_id(1)

    @pl.when(i % tiles_per_batch == 0)
    def _():
        ug_ref[0:SUBLANES, :] = jnp.zeros((SUBLANES, ug_ref.shape[1]), F32)
        uv_ref[0:SUBLANES, :] = jnp.zeros((SUBLANES, uv_ref.shape[1]), F32)

    x = x_ref[...]
    ug_ref[SUBLANES:SUBLANES + bm, :] = jnp.dot(x, wg_ref[...], preferred_element_type=F32)
    uv_ref[SUBLANES:SUBLANES + bm, :] = jnp.dot(x, wv_ref[...], preferred_element_type=F32)

    def conv(u_ref, cw_ref, cb_ref):
        acc = cb_ref[...]
        for t in range(CONV_WIDTH):
            start = SUBLANES - (CONV_WIDTH - 1) + t
            acc = acc + cw_ref[t:t + 1, :] * u_ref[start:start + bm, :]
        return acc

    gate = conv(ug_ref, cwg_ref, cbg_ref)
    val = conv(uv_ref, cwv_ref, cbv_ref)
    o_ref[...] = (_silu(gate) * val).astype(o_ref.dtype)
    ug_ref[0:SUBLANES, :] = ug_ref[bm:bm + SUBLANES, :]
    uv_ref[0:SUBLANES, :] = uv_ref[bm:bm + SUBLANES, :]


def _ffn_up(x, w_stack, layer, conv_w, conv_b, bm, l_pad):
    rows, d = x.shape
    f2 = w_stack.shape[-1]
    f = f2 // 2
    bn = _pick_div(f, 512, MXU_COLS)
    nj = f // bn
    return pl.pallas_call(
        functools.partial(_ffn_up_kernel, bm=bm, tiles_per_batch=l_pad // bm),
        grid=(nj, rows // bm),
        in_specs=[pl.BlockSpec((bm, d), lambda j, i: (i, 0)),
                  pl.BlockSpec((None, d, bn), lambda j, i: (layer, 0, j)),
                  pl.BlockSpec((None, d, bn), lambda j, i: (layer, 0, nj + j)),
                  pl.BlockSpec((CONV_WIDTH, bn), lambda j, i: (0, j)),
                  pl.BlockSpec((CONV_WIDTH, bn), lambda j, i: (0, nj + j)),
                  pl.BlockSpec((1, bn), lambda j, i: (0, j)),
                  pl.BlockSpec((1, bn), lambda j, i: (0, nj + j))],
        out_specs=pl.BlockSpec((bm, bn), lambda j, i: (i, j)),
        out_shape=jax.ShapeDtypeStruct((rows, f), BF16),
        scratch_shapes=[pltpu.VMEM((bm + SUBLANES, bn), F32), pltpu.VMEM((bm + SUBLANES, bn), F32)],
        compiler_params=_params(2),
        name="ffn_up_conv_gate",
    )(x, w_stack, w_stack, conv_w, conv_w, conv_b, conv_b)


def _ret_kernel(lg_ref, q_ref, k_ref, v_ref, g_ref, cos_ref, sin_ref, o_ref, s_ref, *, n_chunks, heads):
    c = RET_CHUNK
    dk = RET_DK
    half = dk // 2
    scale = dk ** -0.5
    assert math.frexp(scale)[0] == 0.5, "key scale must be a power of two"

    @pl.when(pl.program_id(2) == 0)
    def _():
        s_ref[...] = jnp.zeros(s_ref.shape, F32)

    ii = lax.broadcasted_iota(jnp.int32, (c, c), 0)
    jj = lax.broadcasted_iota(jnp.int32, (c, c), 1)
    diff = (ii - jj).astype(F32)
    row = lax.broadcasted_iota(jnp.int32, (c, dk), 0).astype(F32)
    decay, w_q, w_k, chunk_decay = [], [], [], []
    for g in range(heads):
        lg = lg_ref[pl.program_id(1) * heads + g]
        decay.append(jnp.where(diff >= 0, jnp.exp(jnp.maximum(diff, 0.0) * lg) * scale, 0.0))
        w_q.append(jnp.exp((row + 1.0) * lg))
        w_k.append(jnp.exp((c - 1.0 - row) * lg) * scale)
        chunk_decay.append(jnp.exp(jnp.full((1, dk), c, F32) * lg))

    def rot(x, cos, sin):
        x1, x2 = x[:, :half], x[:, half:]
        return jnp.concatenate([x1 * cos - x2 * sin, x1 * sin + x2 * cos], axis=-1)

    def body(n, carry):
        r0 = pl.multiple_of(n * c, c)
        rows = pl.ds(r0, c)
        cos = cos_ref[rows, :]
        sin = sin_ref[rows, :]
        for g in range(heads):
            cols = slice(g * dk, (g + 1) * dk)
            q = rot(q_ref[rows, cols].astype(F32), cos, sin)
            k = rot(k_ref[rows, cols].astype(F32), cos, sin)
            v = v_ref[rows, cols].astype(BF16)
            state = s_ref[g]
            scores = lax.dot_general(q.astype(BF16), k.astype(BF16), NT,
                                     preferred_element_type=F32) * decay[g]
            o = jnp.dot(scores.astype(BF16), v, preferred_element_type=F32)
            o = o + jnp.dot((q * w_q[g]).astype(BF16), state.astype(BF16), preferred_element_type=F32)
            s_ref[g] = state * chunk_decay[g] + lax.dot_general(
                (k * w_k[g]).astype(BF16), v, TN, preferred_element_type=F32)
            mu = jnp.mean(o, -1, keepdims=True)
            oc = o - mu
            var = jnp.mean(oc * oc, -1, keepdims=True)
            y = oc * lax.rsqrt(var + LN_EPS) * _silu(g_ref[rows, cols].astype(F32))
            o_ref[rows, cols] = y.astype(o_ref.dtype)
        return carry

    lax.fori_loop(0, n_chunks, body, 0, unroll=_unroll(n_chunks, (11, 3, 2, 1)))


def _retention(proj3, cos, sin, log_gamma, n_heads):
    b, l_pad, _ = proj3.shape
    heads = 2 if n_heads % 2 == 0 else 1
    groups = n_heads // heads
    width = heads * RET_DK
    rb = _pick_div(l_pad, SEQ_ROWS_TARGET, RET_CHUNK)

    def col(section):
        return pl.BlockSpec((None, rb, width), lambda bi, hg, r: (bi, r, section * groups + hg))

    tab = pl.BlockSpec((rb, RET_DK // 2), lambda bi, hg, r: (r, 0))
    return pl.pallas_call(
        functools.partial(_ret_kernel, n_chunks=rb // RET_CHUNK, heads=heads),
        grid=(b, groups, l_pad // rb),
        in_specs=[pl.BlockSpec(memory_space=pltpu.SMEM), col(0), col(1), col(2), col(3), tab, tab],
        out_specs=pl.BlockSpec((None, rb, width), lambda bi, hg, r: (bi, r, hg)),
        out_shape=jax.ShapeDtypeStruct((b, l_pad, n_heads * RET_DK), BF16),
        scratch_shapes=[pltpu.VMEM((heads, RET_DK, RET_DK), F32)],
        compiler_params=_params(3),
        name="retention",
    )(log_gamma, proj3, proj3, proj3, proj3, cos, sin)


def _hgrn_constants():
    t = HGRN_BLOCK
    idx = np.arange(t)
    sums, upper, pair = [], [], []
    for s in HGRN_LEVELS:
        r = (idx // (2 * s)) * 2 * s + s - 1
        up = (idx % (2 * s)) >= s
        tt = idx[None, :]
        if s < SUBLANES:
            sums.append(np.where(up[:, None], (tt > r[:, None]) & (tt <= idx[:, None]),
                                 (tt > idx[:, None]) & (tt <= r[:, None])))
        upper.append(np.broadcast_to(up[:, None], (t, HGRN_DK)))
        same = (idx[:, None] // (2 * s)) == (idx[None, :] // (2 * s))
        pair.append(same & up[:, None] & ~up[None, :])
    sums.append(idx[None, :] <= idx[:, None])
    pair.append(np.eye(t, dtype=bool))
    return (np.tile(np.concatenate(sums, 0), (1, 3)).astype(np.float32), np.stack(upper).astype(np.float32),
            np.tile(np.stack(pair), (1, 1, 2)).astype(np.float32))


def _hgrn_kernel(q_ref, f_ref, v_ref, g_ref, loglb_ref, l1mlb_ref, ng_ref, sums_ref, upper_ref,
                 pair_ref, o_ref, st_ref, *, n_blocks, heads):
    t = HGRN_BLOCK
    dk = HGRN_DK
    nl = len(HGRN_LEVELS)
    n_mxu = sum(1 for s in HGRN_LEVELS if s < SUBLANES)

    @pl.when(pl.program_id(2) == 0)
    def _():
        st_ref[...] = jnp.zeros(st_ref.shape, F32)

    log_lb = loglb_ref[...]
    log_1m_lb = l1mlb_ref[...]
    norm_g = ng_ref[...]

    def prepare(n):
        rows = pl.ds(pl.multiple_of(n * t, t), t)
        hf = f_ref[rows, :].astype(F32)
        log_sig = jnp.minimum(hf, 0.0) - jnp.log(1.0 + jnp.exp(-jnp.abs(hf)))
        b2 = log_1m_lb + log_sig
        logf = jnp.maximum(log_lb, b2) + jnp.log(1.0 + jnp.exp(-jnp.abs(log_lb - b2)))
        k_all = jnp.exp(b2 - hf)
        q_all = _silu(q_ref[rows, :].astype(F32))

        logf = logf * LOG2_E
        f_hi = logf.astype(BF16)
        rem = logf - f_hi.astype(F32)
        f_mid = rem.astype(BF16)
        f_lo = (rem - f_mid.astype(F32)).astype(BF16)
        e_all = jnp.dot(sums_ref[...], jnp.concatenate([f_hi, f_mid, f_lo], axis=0),
                        preferred_element_type=F32)

        z_out = [[None] * heads for _ in range(nl)]
        qk_out = [[None] * heads for _ in range(4)]
        dec_out = [None] * heads
        for g in range(heads):
            cols = slice(g * dk, (g + 1) * dk)
            q, k = q_all[:, cols], k_all[:, cols]
            prefix = e_all[n_mxu * t:(n_mxu + 1) * t, cols]
            for lvl, s in enumerate(HGRN_LEVELS):
                if lvl < n_mxu:
                    ex = e_all[lvl * t:(lvl + 1) * t, cols]
                    src = jnp.where(upper_ref[lvl] > 0.0, q, k)
                else:
                    ex_parts, src_parts = [], []
                    for r0 in range(0, t, 2 * s):
                        lo, hi = slice(r0, r0 + s), slice(r0 + s, r0 + 2 * s)
                        bound = prefix[r0 + s - 1:r0 + s, :]
                        ex_parts += [bound - prefix[lo, :], prefix[hi, :] - bound]
                        src_parts += [k[lo, :], q[hi, :]]
                    ex = jnp.concatenate(ex_parts, axis=0)
                    src = jnp.concatenate(src_parts, axis=0)
                z_out[lvl][g] = (src * jnp.exp2(ex)).astype(BF16)

            total = prefix[t - 1:t, :]
            qk_out[0][g] = q.astype(BF16)
            qk_out[1][g] = k.astype(BF16)
            qk_out[2][g] = (q * jnp.exp2(prefix)).astype(BF16)
            qk_out[3][g] = (k * jnp.exp2(total - prefix)).astype(BF16)
            dec_out[g] = jnp.exp2(total)

        def cat(parts):
            return jnp.concatenate(parts, axis=1)

        return (tuple(cat(zl) for zl in z_out), tuple(cat(x) for x in qk_out), cat(dec_out))

    zero_tile = jnp.zeros((t, dk), BF16)

    def block_diag(x):
        return jnp.concatenate([jnp.concatenate([x[:, :dk], zero_tile], axis=1),
                                jnp.concatenate([zero_tile, x[:, dk:]], axis=1)], axis=0)

    def consume(n, prepared):
        z_all, qk_all, dec_all = prepared
        rows = pl.ds(pl.multiple_of(n * t, t), t)
        for p in range(heads // 2):
            cols2 = slice(2 * p * dk, 2 * (p + 1) * dk)
            a = pair_ref[nl] * lax.dot_general(qk_all[0][:, cols2], block_diag(qk_all[1][:, cols2]), NT,
                                               preferred_element_type=F32).astype(BF16)
            for lvl in range(nl):
                z = z_all[lvl][:, cols2]
                a = a + pair_ref[lvl] * lax.dot_general(z, block_diag(z), NT,
                                                        preferred_element_type=F32).astype(BF16)
            v = v_ref[rows, cols2].astype(BF16)
            st = [st_ref[2 * p], st_ref[2 * p + 1]]
            st_pair = jnp.concatenate(st, axis=1).astype(BF16)
            o = jnp.dot(a, block_diag(v), preferred_element_type=F32)
            o = o + lax.dot_general(qk_all[2][:, cols2], block_diag(st_pair), NT, preferred_element_type=F32)
            for h in range(2):
                cols = slice((2 * p + h) * dk, (2 * p + h + 1) * dk)
                half = slice(h * dk, (h + 1) * dk)
                st_ref[2 * p + h] = st[h] * dec_all[:, cols] + lax.dot_general(
                    v[:, half], qk_all[3][:, cols], TN, preferred_element_type=F32)
                oh = o[:, half]
                y = oh * lax.rsqrt(jnp.mean(oh * oh, -1, keepdims=True) + NORM_EPS)
                gate = norm_g[:, cols] * _silu(g_ref[rows, cols].astype(F32))
                o_ref[rows, cols] = (y * gate).astype(o_ref.dtype)

    def body(n, prepared):
        consume(n, prepared)
        return prepare(jnp.minimum(n + 1, n_blocks - 1))

    lax.fori_loop(0, n_blocks, body, prepare(0), unroll=_unroll(n_blocks, (11, 3, 2, 1)))


def _hgrn(proj3, col_off, log_lb, log_1m_lb, norm_g, n_heads):
    b, l_pad, _ = proj3.shape
    sums, upper, pair = _hgrn_constants()
    heads = next(h for h in (4, 2) if n_heads % h == 0)
    groups = n_heads // heads
    width = heads * HGRN_DK
    base = col_off // width
    rb = _pick_div(l_pad, SEQ_ROWS_TARGET, HGRN_BLOCK)

    def col(section):
        return pl.BlockSpec((None, rb, width), lambda bi, hg, r: (bi, r, base + section * groups + hg))

    vec = pl.BlockSpec((1, width), lambda bi, hg, r: (0, hg))

    def const(shape):
        return pl.BlockSpec(shape, lambda bi, hg, r: (0,) * len(shape))

    return pl.pallas_call(
        functools.partial(_hgrn_kernel, n_blocks=rb // HGRN_BLOCK, heads=heads),
        grid=(b, groups, l_pad // rb),
        in_specs=[col(0), col(1), col(2), col(3), vec, vec, vec,
                  const(sums.shape), const(upper.shape), const(pair.shape)],
        out_specs=pl.BlockSpec((None, rb, width), lambda bi, hg, r: (bi, r, hg)),
        out_shape=jax.ShapeDtypeStruct((b, l_pad, n_heads * HGRN_DK), BF16),
        scratch_shapes=[pltpu.VMEM((heads, HGRN_DK, HGRN_DK), F32)],
        compiler_params=_params(3),
        name="hgrn2",
    )(proj3, proj3, proj3, proj3, log_lb, log_1m_lb, norm_g,
      jnp.asarray(sums, BF16), jnp.asarray(upper), jnp.asarray(pair, BF16))


def kernel(x, meta_tokens, emb_ln_g, emb_ln_b, hgrn_lb, w_in, b_gate, w_ret_out, w_hgrn_out, w_o,
           hgrn_norm_g, ln1_g, ln1_b, w_up, conv_w, conv_b, w_down, ln2_g, ln2_b):
    batch, seq, d = x.shape
    depth = w_in.shape[0]
    n_meta = meta_tokens.shape[0]
    l_valid = n_meta + seq
    l_pad = -(-l_valid // LANES) * LANES
    rows = batch * l_pad
    ret_w = w_ret_out.shape[1]
    hgrn_w = w_hgrn_out.shape[1]
    ret_heads = ret_w // RET_DK
    hgrn_heads = hgrn_w // HGRN_DK
    in_cols = w_in.shape[-1]
    assert in_cols == 4 * ret_w + 4 * hgrn_w + 2 * d
    alpha = float((2 * depth) ** 0.25)
    bm = _pick_div(l_pad, 1056, BF16_ROWS)
    bm_down = _pick_div(l_pad, max(2 * bm // 3, BF16_ROWS), BF16_ROWS)

    meta = jnp.broadcast_to(meta_tokens[None].astype(x.dtype), (batch, n_meta, d))
    pad = jnp.zeros((batch, l_pad - l_valid, d), x.dtype)
    h0 = jnp.concatenate([meta, x, pad], axis=1).reshape(rows, d)
    xf, xb = _layer_norm(h0, emb_ln_g, emb_ln_b)

    pos = jnp.arange(l_pad, dtype=F32)
    half = RET_DK // 2
    inv = ROPE_BASE ** (-jnp.arange(half, dtype=F32) / half)
    ang = pos[:, None] * inv[None, :]
    cos, sin = jnp.cos(ang), jnp.sin(ang)
    log_gamma = jnp.log1p(-jnp.exp2(-5.0 - jnp.arange(ret_heads, dtype=F32)))

    lb_all = jnp.cumsum(jax.nn.softmax(hgrn_lb.astype(F32), axis=0), axis=0)
    lb_all = lb_all - lb_all[0:1]
    log_lb_all = jnp.log(lb_all)
    log_1m_lb_all = jnp.log1p(-lb_all)

    w_in_l = w_in[0:1].astype(BF16)

    for layer in range(depth):
        casts = [(w, layer) for w in (w_up, w_down, w_ret_out, w_hgrn_out, w_o)]
        casts += [(w_in, layer + 1)] if layer + 1 < depth else []
        proj, cast = _in_proj(xb, w_in_l, 0, bm, casts)
        w_up_l, w_down_l, w_ret_l, w_hgrn_l, w_o_l = cast[:5]
        w_in_l = cast[5] if layer + 1 < depth else None
        proj3 = proj.reshape(batch, l_pad, in_cols)
        ro = _retention(proj3, cos, sin, log_gamma, ret_heads).reshape(rows, ret_w)
        ho = _hgrn(proj3, 4 * ret_w, log_lb_all[layer:layer + 1], log_1m_lb_all[layer:layer + 1],
                   hgrn_norm_g[layer:layer + 1], hgrn_heads).reshape(rows, hgrn_w)
        merged = _merge(ro, ho, w_ret_l, w_hgrn_l, 0, proj, 4 * ret_w + 4 * hgrn_w,
                        b_gate[layer:layer + 1], bm)
        z1 = _matmul_resid(merged, w_o_l, 0, xf, alpha, bm)
        x1f, x1b = _layer_norm(z1, ln1_g[layer], ln1_b[layer])
        act = _ffn_up(x1b, w_up_l, 0, conv_w[layer], conv_b[layer:layer + 1], bm, l_pad)
        z2 = _matmul_resid(act, w_down_l, 0, x1f, alpha, bm_down)
        xf, xb = _layer_norm(z2, ln2_g[layer], ln2_b[layer])

    return xf.reshape(batch, l_pad, d)[:, n_meta:l_valid]
```

```python
import functools
import math

import numpy as np
import jax
import jax.numpy as jnp
from jax import lax
from jax.experimental import pallas as pl
from jax.experimental.pallas import tpu as pltpu

F32 = jnp.float32
BF16 = jnp.bfloat16

N_META = 16
RET_DK = 256
RET_CHUNK = 128
HGRN_DK = 128
HGRN_BLOCK = 128
HGRN_LEVELS = (1, 2, 4, 8, 16, 32, 64)
CONV_WIDTH = 3
ROPE_BASE = 10000.0
LN_EPS = 1e-5
NORM_EPS = 1e-6
LOG2_E = 1.4426950408889634

V7X_VMEM_BYTES = 64 * 1024 * 1024
VMEM_LIMIT = V7X_VMEM_BYTES - 4 * 1024 * 1024
SUBLANES = 8
LANES = 128
MXU_COLS = 256
BF16_ROWS = 16
SEQ_ROWS_TARGET = 8192

NT = (((1,), (1,)), ((), ()))
TN = (((0,), (0,)), ((), ()))


def _params(n_axes):
    return pltpu.CompilerParams(dimension_semantics=("arbitrary",) * n_axes,
                                vmem_limit_bytes=VMEM_LIMIT)


def _pick_div(n, target, mult):
    best = 0
    for d in range(mult, min(n, target) + 1, mult):
        if n % d == 0:
            best = d
    assert best > 0, (n, target, mult)
    return best


def _widest_bn(n, vmem_bytes):
    return next(c for c in (1024, 512, MXU_COLS) if n % c == 0 and vmem_bytes(c) <= VMEM_LIMIT)


def _unroll(trips, choices=(3, 2, 1)):
    return next(u for u in choices if trips % u == 0)


def _sigmoid(x):
    return 0.5 + 0.5 * jnp.tanh(0.5 * x)


def _silu(x):
    h = 0.5 * x
    return h + h * jnp.tanh(h)


def _ln_kernel(x_ref, g_ref, b_ref, of_ref, ob_ref):
    x = x_ref[...]
    mu = jnp.mean(x, -1, keepdims=True)
    xc = x - mu
    var = jnp.mean(xc * xc, -1, keepdims=True)
    y = xc * lax.rsqrt(var + LN_EPS) * g_ref[...] + b_ref[...]
    of_ref[...] = y
    ob_ref[...] = y.astype(BF16)


def _layer_norm(x2d, g, b):
    rows, d = x2d.shape
    bm = _pick_div(rows, 528, BF16_ROWS)
    row_spec = pl.BlockSpec((bm, d), lambda i: (i, 0))
    vec_spec = pl.BlockSpec((1, d), lambda i: (0, 0))
    return pl.pallas_call(
        _ln_kernel,
        grid=(rows // bm,),
        in_specs=[row_spec, vec_spec, vec_spec],
        out_specs=[row_spec, row_spec],
        out_shape=[jax.ShapeDtypeStruct((rows, d), F32), jax.ShapeDtypeStruct((rows, d), BF16)],
        compiler_params=_params(1),
        name="layer_norm",
    )(x2d, g.reshape(1, d), b.reshape(1, d))


def _in_proj_kernel(a_ref, w_ref, *refs, n_cast):
    cast_in, o_ref, cast_out = refs[:n_cast], refs[n_cast], refs[n_cast + 1:]
    o_ref[...] = jnp.dot(a_ref[...], w_ref[...], preferred_element_type=F32).astype(o_ref.dtype)
    for src, dst in zip(cast_in, cast_out):
        dst[...] = src[...].astype(dst.dtype)


def _cast_rows(k, steps):
    return next(r for r in range(BF16_ROWS, k + 1, BF16_ROWS) if k % r == 0 and k // r <= steps)


def _in_proj(a, w_stack, layer, bm, casts):
    rows, k = a.shape
    n = w_stack.shape[-1]
    bn = _pick_div(n, 1024, LANES)
    nj = n // bn
    steps = (rows // bm) * nj
    cast_in_specs, cast_out_specs, cast_shapes = [], [], []
    for w, lw in casts:
        kw, nw = w.shape[1:]
        rb = _cast_rows(kw, steps)
        last = kw // rb - 1
        cast_in_specs.append(pl.BlockSpec(
            (None, rb, nw), lambda i, j, lw=lw, last=last: (lw, jnp.minimum(i * nj + j, last), 0)))
        cast_out_specs.append(pl.BlockSpec(
            (None, rb, nw), lambda i, j, last=last: (0, jnp.minimum(i * nj + j, last), 0)))
        cast_shapes.append(jax.ShapeDtypeStruct((1, kw, nw), BF16))
    outs = pl.pallas_call(
        functools.partial(_in_proj_kernel, n_cast=len(casts)),
        grid=(rows // bm, nj),
        in_specs=[pl.BlockSpec((bm, k), lambda i, j: (i, 0)),
                  pl.BlockSpec((None, k, bn), lambda i, j: (layer, 0, j))] + cast_in_specs,
        out_specs=[pl.BlockSpec((bm, bn), lambda i, j: (i, j))] + cast_out_specs,
        out_shape=[jax.ShapeDtypeStruct((rows, n), BF16)] + cast_shapes,
        compiler_params=_params(2),
        name="in_proj",
    )(a, w_stack, *[w for w, _ in casts])
    return outs[0], outs[1:]


def _mm_resid_kernel(a_ref, w_ref, r_ref, o_ref, *, alpha):
    o_ref[...] = alpha * r_ref[...] + jnp.dot(a_ref[...], w_ref[...], preferred_element_type=F32)


def _matmul_resid(a, w_stack, layer, resid, alpha, bm):
    rows, k = a.shape
    n = w_stack.shape[-1]
    bn = _widest_bn(n, lambda c: 4 * bm * k + 4 * k * c + 20 * bm * c)
    return pl.pallas_call(
        functools.partial(_mm_resid_kernel, alpha=alpha),
        grid=(rows // bm, n // bn),
        in_specs=[pl.BlockSpec((bm, k), lambda i, j: (i, 0)),
                  pl.BlockSpec((None, k, bn), lambda i, j: (layer, 0, j)),
                  pl.BlockSpec((bm, bn), lambda i, j: (i, j))],
        out_specs=pl.BlockSpec((bm, bn), lambda i, j: (i, j)),
        out_shape=jax.ShapeDtypeStruct((rows, n), F32),
        compiler_params=_params(2),
        name="proj_resid",
    )(a, w_stack, resid)


def _merge_kernel(ro_ref, ho_ref, wr_ref, wh_ref, gr_ref, gh_ref, bgr_ref, bgh_ref, o_ref):
    r = jnp.dot(ro_ref[...], wr_ref[...], preferred_element_type=F32)
    h = jnp.dot(ho_ref[...], wh_ref[...], preferred_element_type=F32)
    g_r = _sigmoid(gr_ref[...].astype(F32) + bgr_ref[...])
    g_h = _sigmoid(gh_ref[...].astype(F32) + bgh_ref[...])
    o_ref[...] = (g_r * r + g_h * h).astype(o_ref.dtype)


def _merge(ro, ho, wr_stack, wh_stack, layer, proj, gate_off, b_gate, bm):
    rows, rw = ro.shape
    hw = ho.shape[1]
    d = wr_stack.shape[-1]
    bn = _widest_bn(math.gcd(d, gate_off), lambda c: 4 * bm * (rw + hw) + 4 * (rw + hw) * c + 20 * bm * c)
    nj = d // bn
    goff = gate_off // bn
    return pl.pallas_call(
        _merge_kernel,
        grid=(rows // bm, nj),
        in_specs=[pl.BlockSpec((bm, rw), lambda i, j: (i, 0)),
                  pl.BlockSpec((bm, hw), lambda i, j: (i, 0)),
                  pl.BlockSpec((None, rw, bn), lambda i, j: (layer, 0, j)),
                  pl.BlockSpec((None, hw, bn), lambda i, j: (layer, 0, j)),
                  pl.BlockSpec((bm, bn), lambda i, j: (i, goff + j)),
                  pl.BlockSpec((bm, bn), lambda i, j: (i, goff + nj + j)),
                  pl.BlockSpec((1, bn), lambda i, j: (0, j)),
                  pl.BlockSpec((1, bn), lambda i, j: (0, nj + j))],
        out_specs=pl.BlockSpec((bm, bn), lambda i, j: (i, j)),
        out_shape=jax.ShapeDtypeStruct((rows, d), BF16),
        compiler_params=_params(2),
        name="gated_merge",
    )(ro, ho, wr_stack, wh_stack, proj, proj, b_gate, b_gate)


def _ffn_up_kernel(x_ref, wg_ref, wv_ref, cwg_ref, cwv_ref, cbg_ref, cbv_ref, o_ref,
                   ug_ref, uv_ref, *, bm, tiles_per_batch):
    i = pl.program_id(1)

    @pl.when(i % tiles_per_batch == 0)
    def _():
        ug_ref[0:SUBLANES, :] = jnp.zeros((SUBLANES, ug_ref.shape[1]), F32)
        uv_ref[0:SUBLANES, :] = jnp.zeros((SUBLANES, uv_ref.shape[1]), F32)

    x = x_ref[...]
    ug_ref[SUBLANES:SUBLANES + bm, :] = jnp.dot(x, wg_ref[...], preferred_element_type=F32)
    uv_ref[SUBLANES:SUBLANES + bm, :] = jnp.dot(x, wv_ref[...], preferred_element_type=F32)

    def conv(u_ref, cw_ref, cb_ref):
        u = u_ref[...]
        acc = cb_ref[...]
        for t in range(CONV_WIDTH):
            back = CONV_WIDTH - 1 - t
            shifted = u if back == 0 else pltpu.roll(u, back, axis=0)
            acc = acc + cw_ref[t:t + 1, :] * shifted[SUBLANES:SUBLANES + bm, :]
        return acc

    gate = conv(ug_ref, cwg_ref, cbg_ref)
    val = conv(uv_ref, cwv_ref, cbv_ref)
    o_ref[...] = (_silu(gate) * val).astype(o_ref.dtype)
    ug_ref[0:SUBLANES, :] = ug_ref[bm:bm + SUBLANES, :]
    uv_ref[0:SUBLANES, :] = uv_ref[bm:bm + SUBLANES, :]


def _ffn_up(x, w_stack, layer, conv_w, conv_b, bm, l_pad):
    rows, d = x.shape
    f2 = w_stack.shape[-1]
    f = f2 // 2
    bn = _pick_div(f, 512, MXU_COLS)
    nj = f // bn
    return pl.pallas_call(
        functools.partial(_ffn_up_kernel, bm=bm, tiles_per_batch=l_pad // bm),
        grid=(nj, rows // bm),
        in_specs=[pl.BlockSpec((bm, d), lambda j, i: (i, 0)),
                  pl.BlockSpec((None, d, bn), lambda j, i: (layer, 0, j)),
                  pl.BlockSpec((None, d, bn), lambda j, i: (layer, 0, nj + j)),
                  pl.BlockSpec((CONV_WIDTH, bn), lambda j, i: (0, j)),
                  pl.BlockSpec((CONV_WIDTH, bn), lambda j, i: (0, nj + j)),
                  pl.BlockSpec((1, bn), lambda j, i: (0, j)),
                  pl.BlockSpec((1, bn), lambda j, i: (0, nj + j))],
        out_specs=pl.BlockSpec((bm, bn), lambda j, i: (i, j)),
        out_shape=jax.ShapeDtypeStruct((rows, f), BF16),
        scratch_shapes=[pltpu.VMEM((bm + SUBLANES, bn), F32), pltpu.VMEM((bm + SUBLANES, bn), F32)],
        compiler_params=_params(2),
        name="ffn_up_conv_gate",
    )(x, w_stack, w_stack, conv_w, conv_w, conv_b, conv_b)


def _ret_kernel(lg_ref, q_ref, k_ref, v_ref, g_ref, cos_ref, sin_ref, o_ref, s_ref, *, n_chunks, heads):
    c = RET_CHUNK
    dk = RET_DK
    half = dk // 2
    scale = dk ** -0.5
    assert math.frexp(scale)[0] == 0.5, "key scale must be a power of two"

    @pl.when(pl.program_id(2) == 0)
    def _():
        s_ref[...] = jnp.zeros(s_ref.shape, F32)

    ii = lax.broadcasted_iota(jnp.int32, (c, c), 0)
    jj = lax.broadcasted_iota(jnp.int32, (c, c), 1)
    diff = (ii - jj).astype(F32)
    row = lax.broadcasted_iota(jnp.int32, (c, dk), 0).astype(F32)
    decay, w_q, w_k, chunk_decay = [], [], [], []
    for g in range(heads):
        lg = lg_ref[pl.program_id(1) * heads + g]
        decay.append(jnp.where(diff >= 0, jnp.exp(jnp.maximum(diff, 0.0) * lg) * scale, 0.0))
        w_q.append(jnp.exp((row + 1.0) * lg))
        w_k.append(jnp.exp((c - 1.0 - row) * lg) * scale)
        chunk_decay.append(jnp.exp(jnp.full((1, dk), c, F32) * lg))

    def rot(x, cos, sin):
        x1, x2 = x[:, :half], x[:, half:]
        return jnp.concatenate([x1 * cos - x2 * sin, x1 * sin + x2 * cos], axis=-1)

    def body(n, carry):
        r0 = pl.multiple_of(n * c, c)
        rows = pl.ds(r0, c)
        cos = cos_ref[rows, :]
        sin = sin_ref[rows, :]
        for g in range(heads):
            cols = slice(g * dk, (g + 1) * dk)
            q = rot(q_ref[rows, cols].astype(F32), cos, sin)
            k = rot(k_ref[rows, cols].astype(F32), cos, sin)
            v = v_ref[rows, cols].astype(BF16)
            state = s_ref[g]
            scores = lax.dot_general(q.astype(BF16), k.astype(BF16), NT,
                                     preferred_element_type=F32) * decay[g]
            o = jnp.dot(scores.astype(BF16), v, preferred_element_type=F32)
            o = o + jnp.dot((q * w_q[g]).astype(BF16), state.astype(BF16), preferred_element_type=F32)
            s_ref[g] = state * chunk_decay[g] + lax.dot_general(
                (k * w_k[g]).astype(BF16), v, TN, preferred_element_type=F32)
            mu = jnp.mean(o, -1, keepdims=True)
            oc = o - mu
            var = jnp.mean(oc * oc, -1, keepdims=True)
            y = oc * lax.rsqrt(var + LN_EPS) * _silu(g_ref[rows, cols].astype(F32))
            o_ref[rows, cols] = y.astype(o_ref.dtype)
        return carry

    lax.fori_loop(0, n_chunks, body, 0, unroll=_unroll(n_chunks, (11, 3, 2, 1)))


def _retention(proj3, cos, sin, log_gamma, n_heads):
    b, l_pad, _ = proj3.shape
    heads = 2 if n_heads % 2 == 0 else 1
    groups = n_heads // heads
    width = heads * RET_DK
    rb = _pick_div(l_pad, SEQ_ROWS_TARGET, RET_CHUNK)

    def col(section):
        return pl.BlockSpec((None, rb, width), lambda bi, hg, r: (bi, r, section * groups + hg))

    tab = pl.BlockSpec((rb, RET_DK // 2), lambda bi, hg, r: (r, 0))
    return pl.pallas_call(
        functools.partial(_ret_kernel, n_chunks=rb // RET_CHUNK, heads=heads),
        grid=(b, groups, l_pad // rb),
        in_specs=[pl.BlockSpec(memory_space=pltpu.SMEM), col(0), col(1), col(2), col(3), tab, tab],
        out_specs=pl.BlockSpec((None, rb, width), lambda bi, hg, r: (bi, r, hg)),
        out_shape=jax.ShapeDtypeStruct((b, l_pad, n_heads * RET_DK), BF16),
        scratch_shapes=[pltpu.VMEM((heads, RET_DK, RET_DK), F32)],
        compiler_params=_params(3),
        name="retention",
    )(log_gamma, proj3, proj3, proj3, proj3, cos, sin)


def _hgrn_constants():
    t = HGRN_BLOCK
    idx = np.arange(t)
    sums, upper, pair = [], [], []
    for s in HGRN_LEVELS:
        r = (idx // (2 * s)) * 2 * s + s - 1
        up = (idx % (2 * s)) >= s
        tt = idx[None, :]
        if s < SUBLANES:
            sums.append(np.where(up[:, None], (tt > r[:, None]) & (tt <= idx[:, None]),
                                 (tt > idx[:, None]) & (tt <= r[:, None])))
        upper.append(np.broadcast_to(up[:, None], (t, HGRN_DK)))
        same = (idx[:, None] // (2 * s)) == (idx[None, :] // (2 * s))
        pair.append(same & up[:, None] & ~up[None, :])
    sums.append(idx[None, :] <= idx[:, None])
    pair.append(np.eye(t, dtype=bool))
    return (np.tile(np.concatenate(sums, 0), (1, 3)).astype(np.float32), np.stack(upper).astype(np.float32),
            np.tile(np.stack(pair), (1, 1, 2)).astype(np.float32))


def _hgrn_kernel(q_ref, f_ref, v_ref, g_ref, loglb_ref, l1mlb_ref, ng_ref, sums_ref, upper_ref,
                 pair_ref, o_ref, st_ref, *, n_blocks, heads):
    t = HGRN_BLOCK
    dk = HGRN_DK
    nl = len(HGRN_LEVELS)
    n_mxu = sum(1 for s in HGRN_LEVELS if s < SUBLANES)

    @pl.when(pl.program_id(2) == 0)
    def _():
        st_ref[...] = jnp.zeros(st_ref.shape, F32)

    log_lb = loglb_ref[...]
    log_1m_lb = l1mlb_ref[...]
    norm_g = ng_ref[...]

    def prepare(n):
        rows = pl.ds(pl.multiple_of(n * t, t), t)
        hf = f_ref[rows, :].astype(F32)
        log_sig = jnp.minimum(hf, 0.0) - jnp.log(1.0 + jnp.exp(-jnp.abs(hf)))
        b2 = log_1m_lb + log_sig
        logf = jnp.maximum(log_lb, b2) + jnp.log(1.0 + jnp.exp(-jnp.abs(log_lb - b2)))
        k_all = jnp.exp(b2 - hf)
        q_all = _silu(q_ref[rows, :].astype(F32))

        logf = logf * LOG2_E
        f_hi = logf.astype(BF16)
        rem = logf - f_hi.astype(F32)
        f_mid = rem.astype(BF16)
        f_lo = (rem - f_mid.astype(F32)).astype(BF16)
        e_all = jnp.dot(sums_ref[...], jnp.concatenate([f_hi, f_mid, f_lo], axis=0),
                        preferred_element_type=F32)

        z_out = [[None] * heads for _ in range(nl)]
        qk_out = [[None] * heads for _ in range(4)]
        dec_out = [None] * heads
        for g in range(heads):
            cols = slice(g * dk, (g + 1) * dk)
            q, k = q_all[:, cols], k_all[:, cols]
            prefix = e_all[n_mxu * t:(n_mxu + 1) * t, cols]
            for lvl, s in enumerate(HGRN_LEVELS):
                if lvl < n_mxu:
                    ex = e_all[lvl * t:(lvl + 1) * t, cols]
                    src = jnp.where(upper_ref[lvl] > 0.0, q, k)
                else:
                    ex_parts, src_parts = [], []
                    for r0 in range(0, t, 2 * s):
                        lo, hi = slice(r0, r0 + s), slice(r0 + s, r0 + 2 * s)
                        bound = prefix[r0 + s - 1:r0 + s, :]
                        ex_parts += [bound - prefix[lo, :], prefix[hi, :] - bound]
                        src_parts += [k[lo, :], q[hi, :]]
                    ex = jnp.concatenate(ex_parts, axis=0)
                    src = jnp.concatenate(src_parts, axis=0)
                z_out[lvl][g] = (src * jnp.exp2(ex)).astype(BF16)

            total = prefix[t - 1:t, :]
            qk_out[0][g] = q.astype(BF16)
            qk_out[1][g] = k.astype(BF16)
            qk_out[2][g] = (q * jnp.exp2(prefix)).astype(BF16)
            qk_out[3][g] = (k * jnp.exp2(total - prefix)).astype(BF16)
            dec_out[g] = jnp.exp2(total)

        def cat(parts):
            return jnp.concatenate(parts, axis=1)

        return (tuple(cat(zl) for zl in z_out), tuple(cat(x) for x in qk_out), cat(dec_out))

    zero_tile = jnp.zeros((t, dk), BF16)

    def block_diag(x):
        return jnp.concatenate([jnp.concatenate([x[:, :dk], zero_tile], axis=1),
                                jnp.concatenate([zero_tile, x[:, dk:]], axis=1)], axis=0)

    def consume(n, prepared):
        z_all, qk_all, dec_all = prepared
        rows = pl.ds(pl.multiple_of(n * t, t), t)
        for p in range(heads // 2):
            cols2 = slice(2 * p * dk, 2 * (p + 1) * dk)
            a = pair_ref[nl] * lax.dot_general(qk_all[0][:, cols2], block_diag(qk_all[1][:, cols2]), NT,
                                               preferred_element_type=F32).astype(BF16)
            for lvl in range(nl):
                z = z_all[lvl][:, cols2]
                a = a + pair_ref[lvl] * lax.dot_general(z, block_diag(z), NT,
                                                        preferred_element_type=F32).astype(BF16)
            v = v_ref[rows, cols2].astype(BF16)
            st = [st_ref[2 * p], st_ref[2 * p + 1]]
            st_pair = jnp.concatenate(st, axis=1).astype(BF16)
            o = jnp.dot(a, block_diag(v), preferred_element_type=F32)
            o = o + lax.dot_general(qk_all[2][:, cols2], block_diag(st_pair), NT, preferred_element_type=F32)
            for h in range(2):
                cols = slice((2 * p + h) * dk, (2 * p + h + 1) * dk)
                half = slice(h * dk, (h + 1) * dk)
                st_ref[2 * p + h] = st[h] * dec_all[:, cols] + lax.dot_general(
                    v[:, half], qk_all[3][:, cols], TN, preferred_element_type=F32)
                oh = o[:, half]
                y = oh * lax.rsqrt(jnp.mean(oh * oh, -1, keepdims=True) + NORM_EPS)
                gate = norm_g[:, cols] * _silu(g_ref[rows, cols].astype(F32))
                o_ref[rows, cols] = (y * gate).astype(o_ref.dtype)

    def body(n, prepared):
        consume(n, prepared)
        return prepare(jnp.minimum(n + 1, n_blocks - 1))

    lax.fori_loop(0, n_blocks, body, prepare(0), unroll=_unroll(n_blocks, (11, 3, 2, 1)))


def _hgrn(proj3, col_off, log_lb, log_1m_lb, norm_g, n_heads):
    b, l_pad, _ = proj3.shape
    sums, upper, pair = _hgrn_constants()
    heads = next(h for h in (4, 2) if n_heads % h == 0)
    groups = n_heads // heads
    width = heads * HGRN_DK
    base = col_off // width
    rb = _pick_div(l_pad, SEQ_ROWS_TARGET, HGRN_BLOCK)

    def col(section):
        return pl.BlockSpec((None, rb, width), lambda bi, hg, r: (bi, r, base + section * groups + hg))

    vec = pl.BlockSpec((1, width), lambda bi, hg, r: (0, hg))

    def const(shape):
        return pl.BlockSpec(shape, lambda bi, hg, r: (0,) * len(shape))

    return pl.pallas_call(
        functools.partial(_hgrn_kernel, n_blocks=rb // HGRN_BLOCK, heads=heads),
        grid=(b, groups, l_pad // rb),
        in_specs=[col(0), col(1), col(2), col(3), vec, vec, vec,
                  const(sums.shape), const(upper.shape), const(pair.shape)],
        out_specs=pl.BlockSpec((None, rb, width), lambda bi, hg, r: (bi, r, hg)),
        out_shape=jax.ShapeDtypeStruct((b, l_pad, n_heads * HGRN_DK), BF16),
        scratch_shapes=[pltpu.VMEM((heads, HGRN_DK, HGRN_DK), F32)],
        compiler_params=_params(3),
        name="hgrn2",
    )(proj3, proj3, proj3, proj3, log_lb, log_1m_lb, norm_g,
      jnp.asarray(sums, BF16), jnp.asarray(upper), jnp.asarray(pair, BF16))


def kernel(x, meta_tokens, emb_ln_g, emb_ln_b, hgrn_lb, w_in, b_gate, w_ret_out, w_hgrn_out, w_o,
           hgrn_norm_g, ln1_g, ln1_b, w_up, conv_w, conv_b, w_down, ln2_g, ln2_b):
    batch, seq, d = x.shape
    depth = w_in.shape[0]
    n_meta = meta_tokens.shape[0]
    l_valid = n_meta + seq
    l_pad = -(-l_valid // LANES) * LANES
    rows = batch * l_pad
    ret_w = w_ret_out.shape[1]
    hgrn_w = w_hgrn_out.shape[1]
    ret_heads = ret_w // RET_DK
    hgrn_heads = hgrn_w // HGRN_DK
    in_cols = w_in.shape[-1]
    assert in_cols == 4 * ret_w + 4 * hgrn_w + 2 * d
    alpha = float((2 * depth) ** 0.25)
    bm = _pick_div(l_pad, 1056, BF16_ROWS)
    bm_down = _pick_div(l_pad, max(2 * bm // 3, BF16_ROWS), BF16_ROWS)

    meta = jnp.broadcast_to(meta_tokens[None].astype(x.dtype), (batch, n_meta, d))
    pad = jnp.zeros((batch, l_pad - l_valid, d), x.dtype)
    h0 = jnp.concatenate([meta, x, pad], axis=1).reshape(rows, d)
    xf, xb = _layer_norm(h0, emb_ln_g, emb_ln_b)

    pos = jnp.arange(l_pad, dtype=F32)
    half = RET_DK // 2
    inv = ROPE_BASE ** (-jnp.arange(half, dtype=F32) / half)
    ang = pos[:, None] * inv[None, :]
    cos, sin = jnp.cos(ang), jnp.sin(ang)
    log_gamma = jnp.log1p(-jnp.exp2(-5.0 - jnp.arange(ret_heads, dtype=F32)))

    lb_all = jnp.cumsum(jax.nn.softmax(hgrn_lb.astype(F32), axis=0), axis=0)
    lb_all = lb_all - lb_all[0:1]
    log_lb_all = jnp.log(lb_all)
    log_1m_lb_all = jnp.log1p(-lb_all)

    w_in_l = w_in[0:1].astype(BF16)

    for layer in range(depth):
        casts = [(w, layer) for w in (w_up, w_down, w_ret_out, w_hgrn_out, w_o)]
        casts += [(w_in, layer + 1)] if layer + 1 < depth else []
        proj, cast = _in_proj(xb, w_in_l, 0, bm, casts)
        w_up_l, w_down_l, w_ret_l, w_hgrn_l, w_o_l = cast[:5]
        w_in_l = cast[5] if layer + 1 < depth else None
        proj3 = proj.reshape(batch, l_pad, in_cols)
        ro = _retention(proj3, cos, sin, log_gamma, ret_heads).reshape(rows, ret_w)
        ho = _hgrn(proj3, 4 * ret_w, log_lb_all[layer:layer + 1], log_1m_lb_all[layer:layer + 1],
                   hgrn_norm_g[layer:layer + 1], hgrn_heads).reshape(rows, hgrn_w)
        merged = _merge(ro, ho, w_ret_l, w_hgrn_l, 0, proj, 4 * ret_w + 4 * hgrn_w,
                        b_gate[layer:layer + 1], bm)
        z1 = _matmul_resid(merged, w_o_l, 0, xf, alpha, bm)
        x1f, x1b = _layer_norm(z1, ln1_g[layer], ln1_b[layer])
        act = _ffn_up(x1b, w_up_l, 0, conv_w[layer], conv_b[layer:layer + 1], bm, l_pad)
        z2 = _matmul_resid(act, w_down_l, 0, x1f, alpha, bm_down)
        xf, xb = _layer_norm(z2, ln2_g[layer], ln2_b[layer])

    return xf.reshape(batch, l_pad, d)[:, n_meta:l_valid]
```

```python
import functools
import math

import numpy as np
import jax
import jax.numpy as jnp
from jax import lax
from jax.experimental import pallas as pl
from jax.experimental.pallas import tpu as pltpu

F32 = jnp.float32
BF16 = jnp.bfloat16

RET_DK = 256
RET_CHUNK = 128
HGRN_DK = 128
HGRN_BLOCK = 128
HGRN_LEVELS = (1, 2, 4, 8, 16, 32, 64)
CONV_WIDTH = 3
ROPE_BASE = 10000.0
LN_EPS = 1e-5
NORM_EPS = 1e-6
LOG2_E = 1.4426950408889634

V7X_VMEM_BYTES = 64 * 1024 * 1024
VMEM_LIMIT = V7X_VMEM_BYTES - 4 * 1024 * 1024
SUBLANES = 8
LANES = 128
MXU_COLS = 256
BF16_ROWS = 16
SEQ_ROWS_TARGET = 8192

NT = (((1,), (1,)), ((), ()))
TN = (((0,), (0,)), ((), ()))


def _params(n_axes):
    return pltpu.CompilerParams(dimension_semantics=("arbitrary",) * n_axes,
                                vmem_limit_bytes=VMEM_LIMIT)


def _pick_div(n, target, mult):
    best = 0
    for d in range(mult, min(n, target) + 1, mult):
        if n % d == 0:
            best = d
    assert best > 0, (n, target, mult)
    return best


def _widest_bn(n, vmem_bytes):
    return next(c for c in (1024, 512, MXU_COLS) if n % c == 0 and vmem_bytes(c) <= VMEM_LIMIT)


def _unroll(trips, choices=(3, 2, 1)):
    return next(u for u in choices if trips % u == 0)


def _sigmoid(x):
    return 0.5 + 0.5 * jnp.tanh(0.5 * x)


def _silu(x):
    h = 0.5 * x
    return h + h * jnp.tanh(h)


def _ln_kernel(x_ref, g_ref, b_ref, of_ref, ob_ref):
    x = x_ref[...]
    mu = jnp.mean(x, -1, keepdims=True)
    xc = x - mu
    var = jnp.mean(xc * xc, -1, keepdims=True)
    y = xc * lax.rsqrt(var + LN_EPS) * g_ref[...] + b_ref[...]
    of_ref[...] = y
    ob_ref[...] = y.astype(BF16)


def _layer_norm(x2d, g, b):
    rows, d = x2d.shape
    bm = _pick_div(rows, 528, BF16_ROWS)
    row_spec = pl.BlockSpec((bm, d), lambda i: (i, 0))
    vec_spec = pl.BlockSpec((1, d), lambda i: (0, 0))
    return pl.pallas_call(
        _ln_kernel,
        grid=(rows // bm,),
        in_specs=[row_spec, vec_spec, vec_spec],
        out_specs=[row_spec, row_spec],
        out_shape=[jax.ShapeDtypeStruct((rows, d), F32), jax.ShapeDtypeStruct((rows, d), BF16)],
        compiler_params=_params(1),
        name="layer_norm",
    )(x2d, g.reshape(1, d), b.reshape(1, d))


def _in_proj_kernel(a_ref, w_ref, *refs, n_cast):
    cast_in, o_ref, cast_out = refs[:n_cast], refs[n_cast], refs[n_cast + 1:]
    o_ref[...] = jnp.dot(a_ref[...], w_ref[...], preferred_element_type=F32).astype(o_ref.dtype)
    for src, dst in zip(cast_in, cast_out):
        dst[...] = src[...].astype(dst.dtype)


def _cast_rows(k, steps):
    return next(r for r in range(BF16_ROWS, k + 1, BF16_ROWS) if k % r == 0 and k // r <= steps)


def _in_proj(a, w_stack, layer, bm, casts):
    rows, k = a.shape
    n = w_stack.shape[-1]
    bn = _pick_div(n, 1024, LANES)
    nj = n // bn
    steps = (rows // bm) * nj
    cast_in_specs, cast_out_specs, cast_shapes = [], [], []
    for w, lw in casts:
        kw, nw = w.shape[1:]
        rb = _cast_rows(kw, steps)
        last = kw // rb - 1
        cast_in_specs.append(pl.BlockSpec(
            (None, rb, nw), lambda i, j, lw=lw, last=last: (lw, jnp.minimum(i * nj + j, last), 0)))
        cast_out_specs.append(pl.BlockSpec(
            (None, rb, nw), lambda i, j, last=last: (0, jnp.minimum(i * nj + j, last), 0)))
        cast_shapes.append(jax.ShapeDtypeStruct((1, kw, nw), BF16))
    outs = pl.pallas_call(
        functools.partial(_in_proj_kernel, n_cast=len(casts)),
        grid=(rows // bm, nj),
        in_specs=[pl.BlockSpec((bm, k), lambda i, j: (i, 0)),
                  pl.BlockSpec((None, k, bn), lambda i, j: (layer, 0, j))] + cast_in_specs,
        out_specs=[pl.BlockSpec((bm, bn), lambda i, j: (i, j))] + cast_out_specs,
        out_shape=[jax.ShapeDtypeStruct((rows, n), BF16)] + cast_shapes,
        compiler_params=_params(2),
        name="in_proj",
    )(a, w_stack, *[w for w, _ in casts])
    return outs[0], outs[1:]


def _mm_resid_kernel(a_ref, w_ref, r_ref, o_ref, *, alpha):
    o_ref[...] = alpha * r_ref[...] + jnp.dot(a_ref[...], w_ref[...], preferred_element_type=F32)


def _matmul_resid(a, w_stack, layer, resid, alpha, bm):
    rows, k = a.shape
    n = w_stack.shape[-1]
    bn = _widest_bn(n, lambda c: 4 * bm * k + 4 * k * c + 20 * bm * c)
    return pl.pallas_call(
        functools.partial(_mm_resid_kernel, alpha=alpha),
        grid=(rows // bm, n // bn),
        in_specs=[pl.BlockSpec((bm, k), lambda i, j: (i, 0)),
                  pl.BlockSpec((None, k, bn), lambda i, j: (layer, 0, j)),
                  pl.BlockSpec((bm, bn), lambda i, j: (i, j))],
        out_specs=pl.BlockSpec((bm, bn), lambda i, j: (i, j)),
        out_shape=jax.ShapeDtypeStruct((rows, n), F32),
        compiler_params=_params(2),
        name="proj_resid",
    )(a, w_stack, resid)


def _merge_kernel(ro_ref, ho_ref, wr_ref, wh_ref, gr_ref, gh_ref, bgr_ref, bgh_ref, o_ref):
    r = jnp.dot(ro_ref[...], wr_ref[...], preferred_element_type=F32)
    h = jnp.dot(ho_ref[...], wh_ref[...], preferred_element_type=F32)
    g_r = _sigmoid(gr_ref[...].astype(F32) + bgr_ref[...])
    g_h = _sigmoid(gh_ref[...].astype(F32) + bgh_ref[...])
    o_ref[...] = (g_r * r + g_h * h).astype(o_ref.dtype)


def _merge(ro, ho, wr_stack, wh_stack, layer, proj, gate_off, b_gate, bm):
    rows, rw = ro.shape
    hw = ho.shape[1]
    d = wr_stack.shape[-1]
    bn = _widest_bn(math.gcd(d, gate_off), lambda c: 4 * bm * (rw + hw) + 4 * (rw + hw) * c + 20 * bm * c)
    nj = d // bn
    goff = gate_off // bn
    return pl.pallas_call(
        _merge_kernel,
        grid=(rows // bm, nj),
        in_specs=[pl.BlockSpec((bm, rw), lambda i, j: (i, 0)),
                  pl.BlockSpec((bm, hw), lambda i, j: (i, 0)),
                  pl.BlockSpec((None, rw, bn), lambda i, j: (layer, 0, j)),
                  pl.BlockSpec((None, hw, bn), lambda i, j: (layer, 0, j)),
                  pl.BlockSpec((bm, bn), lambda i, j: (i, goff + j)),
                  pl.BlockSpec((bm, bn), lambda i, j: (i, goff + nj + j)),
                  pl.BlockSpec((1, bn), lambda i, j: (0, j)),
                  pl.BlockSpec((1, bn), lambda i, j: (0, nj + j))],
        out_specs=pl.BlockSpec((bm, bn), lambda i, j: (i, j)),
        out_shape=jax.ShapeDtypeStruct((rows, d), BF16),
        compiler_params=_params(2),
        name="gated_merge",
    )(ro, ho, wr_stack, wh_stack, proj, proj, b_gate, b_gate)


def _ffn_up_kernel(x_ref, wg_ref, wv_ref, cwg_ref, cwv_ref, cbg_ref, cbv_ref, o_ref,
                   ug_ref, uv_ref, *, bm, tiles_per_batch):
    i = pl.program_id(1)

    @pl.when(i % tiles_per_batch == 0)
    def _():
        ug_ref[0:SUBLANES, :] = jnp.zeros((SUBLANES, ug_ref.shape[1]), F32)
        uv_ref[0:SUBLANES, :] = jnp.zeros((SUBLANES, uv_ref.shape[1]), F32)

    x = x_ref[...]
    ug_ref[SUBLANES:SUBLANES + bm, :] = jnp.dot(x, wg_ref[...], preferred_element_type=F32)
    uv_ref[SUBLANES:SUBLANES + bm, :] = jnp.dot(x, wv_ref[...], preferred_element_type=F32)

    def conv(u_ref, cw_ref, cb_ref):
        u = u_ref[...]
        acc = cb_ref[...]
        for t in range(CONV_WIDTH):
            back = CONV_WIDTH - 1 - t
            shifted = u if back == 0 else pltpu.roll(u, back, axis=0)
            acc = acc + cw_ref[t:t + 1, :] * shifted[SUBLANES:SUBLANES + bm, :]
        return acc

    gate = conv(ug_ref, cwg_ref, cbg_ref)
    val = conv(uv_ref, cwv_ref, cbv_ref)
    o_ref[...] = (_silu(gate) * val).astype(o_ref.dtype)
    ug_ref[0:SUBLANES, :] = ug_ref[bm:bm + SUBLANES, :]
    uv_ref[0:SUBLANES, :] = uv_ref[bm:bm + SUBLANES, :]


def _ffn_up(x, w_stack, layer, conv_w, conv_b, bm, l_pad):
    rows, d = x.shape
    f2 = w_stack.shape[-1]
    f = f2 // 2
    bn = _pick_div(f, 512, MXU_COLS)
    nj = f // bn
    return pl.pallas_call(
        functools.partial(_ffn_up_kernel, bm=bm, tiles_per_batch=l_pad // bm),
        grid=(nj, rows // bm),
        in_specs=[pl.BlockSpec((bm, d), lambda j, i: (i, 0)),
                  pl.BlockSpec((None, d, bn), lambda j, i: (layer, 0, j)),
                  pl.BlockSpec((None, d, bn), lambda j, i: (layer, 0, nj + j)),
                  pl.BlockSpec((CONV_WIDTH, bn), lambda j, i: (0, j)),
                  pl.BlockSpec((CONV_WIDTH, bn), lambda j, i: (0, nj + j)),
                  pl.BlockSpec((1, bn), lambda j, i: (0, j)),
                  pl.BlockSpec((1, bn), lambda j, i: (0, nj + j))],
        out_specs=pl.BlockSpec((bm, bn), lambda j, i: (i, j)),
        out_shape=jax.ShapeDtypeStruct((rows, f), BF16),
        scratch_shapes=[pltpu.VMEM((bm + SUBLANES, bn), F32), pltpu.VMEM((bm + SUBLANES, bn), F32)],
        compiler_params=_params(2),
        name="ffn_up_conv_gate",
    )(x, w_stack, w_stack, conv_w, conv_w, conv_b, conv_b)


def _ret_kernel(lg_ref, q_ref, k_ref, v_ref, g_ref, cos_ref, sin_ref, o_ref, s_ref, *, n_chunks, heads):
    c = RET_CHUNK
    dk = RET_DK
    half = dk // 2
    scale = dk ** -0.5
    assert math.frexp(scale)[0] == 0.5, "key scale must be a power of two"

    @pl.when(pl.program_id(2) == 0)
    def _():
        s_ref[...] = jnp.zeros(s_ref.shape, F32)

    ii = lax.broadcasted_iota(jnp.int32, (c, c), 0)
    jj = lax.broadcasted_iota(jnp.int32, (c, c), 1)
    diff = (ii - jj).astype(F32)
    row = lax.broadcasted_iota(jnp.int32, (c, dk), 0).astype(F32)
    decay, w_q, w_k, chunk_decay = [], [], [], []
    for g in range(heads):
        lg = lg_ref[pl.program_id(1) * heads + g]
        decay.append(jnp.where(diff >= 0, jnp.exp(jnp.maximum(diff, 0.0) * lg) * scale, 0.0))
        w_q.append(jnp.exp((row + 1.0) * lg))
        w_k.append(jnp.exp((c - 1.0 - row) * lg) * scale)
        chunk_decay.append(jnp.exp(jnp.full((1, dk), c, F32) * lg))

    def rot(x, cos, sin):
        x1, x2 = x[:, :half], x[:, half:]
        return jnp.concatenate([x1 * cos - x2 * sin, x1 * sin + x2 * cos], axis=-1)

    def body(n, carry):
        r0 = pl.multiple_of(n * c, c)
        rows = pl.ds(r0, c)
        cos = cos_ref[rows, :]
        sin = sin_ref[rows, :]
        for g in range(heads):
            cols = slice(g * dk, (g + 1) * dk)
            q = rot(q_ref[rows, cols].astype(F32), cos, sin)
            k = rot(k_ref[rows, cols].astype(F32), cos, sin)
            v = v_ref[rows, cols].astype(BF16)
            state = s_ref[g]
            scores = lax.dot_general(q.astype(BF16), k.astype(BF16), NT,
                                     preferred_element_type=F32) * decay[g]
            o = jnp.dot(scores.astype(BF16), v, preferred_element_type=F32)
            o = o + jnp.dot((q * w_q[g]).astype(BF16), state.astype(BF16), preferred_element_type=F32)
            s_ref[g] = state * chunk_decay[g] + lax.dot_general(
                (k * w_k[g]).astype(BF16), v, TN, preferred_element_type=F32)
            mu = jnp.mean(o, -1, keepdims=True)
            oc = o - mu
            var = jnp.mean(oc * oc, -1, keepdims=True)
            y = oc * lax.rsqrt(var + LN_EPS) * _silu(g_ref[rows, cols].astype(F32))
            o_ref[rows, cols] = y.astype(o_ref.dtype)
        return carry

    lax.fori_loop(0, n_chunks, body, 0, unroll=_unroll(n_chunks, (11, 3, 2, 1)))


def _retention(proj3, cos, sin, log_gamma, n_heads):
    b, l_pad, _ = proj3.shape
    heads = 2 if n_heads % 2 == 0 else 1
    groups = n_heads // heads
    width = heads * RET_DK
    rb = _pick_div(l_pad, SEQ_ROWS_TARGET, RET_CHUNK)

    def col(section):
        return pl.BlockSpec((None, rb, width), lambda bi, hg, r: (bi, r, section * groups + hg))

    tab = pl.BlockSpec((rb, RET_DK // 2), lambda bi, hg, r: (r, 0))
    return pl.pallas_call(
        functools.partial(_ret_kernel, n_chunks=rb // RET_CHUNK, heads=heads),
        grid=(b, groups, l_pad // rb),
        in_specs=[pl.BlockSpec(memory_space=pltpu.SMEM), col(0), col(1), col(2), col(3), tab, tab],
        out_specs=pl.BlockSpec((None, rb, width), lambda bi, hg, r: (bi, r, hg)),
        out_shape=jax.ShapeDtypeStruct((b, l_pad, n_heads * RET_DK), BF16),
        scratch_shapes=[pltpu.VMEM((heads, RET_DK, RET_DK), F32)],
        compiler_params=_params(3),
        name="retention",
    )(log_gamma, proj3, proj3, proj3, proj3, cos, sin)


def _hgrn_constants():
    t = HGRN_BLOCK
    idx = np.arange(t)
    sums, upper, pair = [], [], []
    for s in HGRN_LEVELS:
        r = (idx // (2 * s)) * 2 * s + s - 1
        up = (idx % (2 * s)) >= s
        tt = idx[None, :]
        if s < SUBLANES:
            sums.append(np.where(up[:, None], (tt > r[:, None]) & (tt <= idx[:, None]),
                                 (tt > idx[:, None]) & (tt <= r[:, None])))
        upper.append(np.broadcast_to(up[:, None], (t, HGRN_DK)))
        same = (idx[:, None] // (2 * s)) == (idx[None, :] // (2 * s))
        pair.append(same & up[:, None] & ~up[None, :])
    sums.append(idx[None, :] <= idx[:, None])
    pair.append(np.eye(t, dtype=bool))
    return (np.tile(np.concatenate(sums, 0), (1, 3)).astype(np.float32), np.stack(upper).astype(np.float32),
            np.tile(np.stack(pair), (1, 1, 2)).astype(np.float32))


def _hgrn_kernel(q_ref, f_ref, v_ref, g_ref, loglb_ref, l1mlb_ref, ng_ref, sums_ref, upper_ref,
                 pair_ref, o_ref, st_ref, *, n_blocks, heads):
    t = HGRN_BLOCK
    dk = HGRN_DK
    nl = len(HGRN_LEVELS)
    n_mxu = sum(1 for s in HGRN_LEVELS if s < SUBLANES)

    @pl.when(pl.program_id(2) == 0)
    def _():
        st_ref[...] = jnp.zeros(st_ref.shape, F32)

    log_lb = loglb_ref[...]
    log_1m_lb = l1mlb_ref[...]
    norm_g = ng_ref[...]

    def prepare(n):
        rows = pl.ds(pl.multiple_of(n * t, t), t)
        hf = f_ref[rows, :].astype(F32)
        log_sig = jnp.minimum(hf, 0.0) - jnp.log(1.0 + jnp.exp(-jnp.abs(hf)))
        b2 = log_1m_lb + log_sig
        logf = jnp.maximum(log_lb, b2) + jnp.log(1.0 + jnp.exp(-jnp.abs(log_lb - b2)))
        k_all = jnp.exp(b2 - hf)
        q_all = _silu(q_ref[rows, :].astype(F32))

        logf = logf * LOG2_E
        f_hi = logf.astype(BF16)
        rem = logf - f_hi.astype(F32)
        f_mid = rem.astype(BF16)
        f_lo = (rem - f_mid.astype(F32)).astype(BF16)
        e_all = jnp.dot(sums_ref[...], jnp.concatenate([f_hi, f_mid, f_lo], axis=0),
                        preferred_element_type=F32)

        z_out = [[None] * heads for _ in range(nl)]
        qk_out = [[None] * heads for _ in range(4)]
        dec_out = [None] * heads
        for g in range(heads):
            cols = slice(g * dk, (g + 1) * dk)
            q, k = q_all[:, cols], k_all[:, cols]
            prefix = e_all[n_mxu * t:(n_mxu + 1) * t, cols]
            for lvl, s in enumerate(HGRN_LEVELS):
                if lvl < n_mxu:
                    ex = e_all[lvl * t:(lvl + 1) * t, cols]
                    src = jnp.where(upper_ref[lvl] > 0.0, q, k)
                else:
                    ex_parts, src_parts = [], []
                    for r0 in range(0, t, 2 * s):
                        lo, hi = slice(r0, r0 + s), slice(r0 + s, r0 + 2 * s)
                        bound = prefix[r0 + s - 1:r0 + s, :]
                        ex_parts += [bound - prefix[lo, :], prefix[hi, :] - bound]
                        src_parts += [k[lo, :], q[hi, :]]
                    ex = jnp.concatenate(ex_parts, axis=0)
                    src = jnp.concatenate(src_parts, axis=0)
                z_out[lvl][g] = (src * jnp.exp2(ex)).astype(BF16)

            total = prefix[t - 1:t, :]
            qk_out[0][g] = q.astype(BF16)
            qk_out[1][g] = k.astype(BF16)
            qk_out[2][g] = (q * jnp.exp2(prefix)).astype(BF16)
            qk_out[3][g] = (k * jnp.exp2(total - prefix)).astype(BF16)
            dec_out[g] = jnp.exp2(total)

        def cat(parts):
            return jnp.concatenate(parts, axis=1)

        return (tuple(cat(zl) for zl in z_out), tuple(cat(x) for x in qk_out), cat(dec_out))

    zero_tile = jnp.zeros((t, dk), BF16)

    def block_diag(x):
        return jnp.concatenate([jnp.concatenate([x[:, :dk], zero_tile], axis=1),
                                jnp.concatenate([zero_tile, x[:, dk:]], axis=1)], axis=0)

    def consume(n, prepared):
        z_all, qk_all, dec_all = prepared
        rows = pl.ds(pl.multiple_of(n * t, t), t)
        for p in range(heads // 2):
            cols2 = slice(2 * p * dk, 2 * (p + 1) * dk)
            a = pair_ref[nl] * lax.dot_general(qk_all[0][:, cols2], block_diag(qk_all[1][:, cols2]), NT,
                                               preferred_element_type=F32).astype(BF16)
            for lvl in range(nl):
                z = z_all[lvl][:, cols2]
                a = a + pair_ref[lvl] * lax.dot_general(z, block_diag(z), NT,
                                                        preferred_element_type=F32).astype(BF16)
            v = v_ref[rows, cols2].astype(BF16)
            st = [st_ref[2 * p], st_ref[2 * p + 1]]
            st_pair = jnp.concatenate(st, axis=1).astype(BF16)
            o = jnp.dot(a, block_diag(v), preferred_element_type=F32)
            o = o + lax.dot_general(qk_all[2][:, cols2], block_diag(st_pair), NT, preferred_element_type=F32)
            for h in range(2):
                cols = slice((2 * p + h) * dk, (2 * p + h + 1) * dk)
                half = slice(h * dk, (h + 1) * dk)
                st_ref[2 * p + h] = st[h] * dec_all[:, cols] + lax.dot_general(
                    v[:, half], qk_all[3][:, cols], TN, preferred_element_type=F32)
                oh = o[:, half]
                y = oh * lax.rsqrt(jnp.mean(oh * oh, -1, keepdims=True) + NORM_EPS)
                gate = norm_g[:, cols] * _silu(g_ref[rows, cols].astype(F32))
                o_ref[rows, cols] = (y * gate).astype(o_ref.dtype)

    def body(n, prepared):
        consume(n, prepared)
        return prepare(jnp.minimum(n + 1, n_blocks - 1))

    lax.fori_loop(0, n_blocks, body, prepare(0), unroll=_unroll(n_blocks, (11, 3, 2, 1)))


def _hgrn(proj3, col_off, log_lb, log_1m_lb, norm_g, n_heads):
    b, l_pad, _ = proj3.shape
    sums, upper, pair = _hgrn_constants()
    heads = next(h for h in (4, 2) if n_heads % h == 0)
    groups = n_heads // heads
    width = heads * HGRN_DK
    base = col_off // width
    rb = _pick_div(l_pad, SEQ_ROWS_TARGET, HGRN_BLOCK)

    def col(section):
        return pl.BlockSpec((None, rb, width), lambda bi, hg, r: (bi, r, base + section * groups + hg))

    vec = pl.BlockSpec((1, width), lambda bi, hg, r: (0, hg))

    def const(shape):
        return pl.BlockSpec(shape, lambda bi, hg, r: (0,) * len(shape))

    return pl.pallas_call(
        functools.partial(_hgrn_kernel, n_blocks=rb // HGRN_BLOCK, heads=heads),
        grid=(b, groups, l_pad // rb),
        in_specs=[col(0), col(1), col(2), col(3), vec, vec, vec,
                  const(sums.shape), const(upper.shape), const(pair.shape)],
        out_specs=pl.BlockSpec((None, rb, width), lambda bi, hg, r: (bi, r, hg)),
        out_shape=jax.ShapeDtypeStruct((b, l_pad, n_heads * HGRN_DK), BF16),
        scratch_shapes=[pltpu.VMEM((heads, HGRN_DK, HGRN_DK), F32)],
        compiler_params=_params(3),
        name="hgrn2",
    )(proj3, proj3, proj3, proj3, log_lb, log_1m_lb, norm_g,
      jnp.asarray(sums, BF16), jnp.asarray(upper), jnp.asarray(pair, BF16))


def kernel(x, meta_tokens, emb_ln_g, emb_ln_b, hgrn_lb, w_in, b_gate, w_ret_out, w_hgrn_out, w_o,
           hgrn_norm_g, ln1_g, ln1_b, w_up, conv_w, conv_b, w_down, ln2_g, ln2_b):
    batch, seq, d = x.shape
    depth = w_in.shape[0]
    n_meta = meta_tokens.shape[0]
    l_valid = n_meta + seq
    l_pad = -(-l_valid // LANES) * LANES
    rows = batch * l_pad
    ret_w = w_ret_out.shape[1]
    hgrn_w = w_hgrn_out.shape[1]
    ret_heads = ret_w // RET_DK
    hgrn_heads = hgrn_w // HGRN_DK
    in_cols = w_in.shape[-1]
    assert in_cols == 4 * ret_w + 4 * hgrn_w + 2 * d
    alpha = float((2 * depth) ** 0.25)
    bm = _pick_div(l_pad, 1056, BF16_ROWS)

    meta = jnp.broadcast_to(meta_tokens[None].astype(x.dtype), (batch, n_meta, d))
    pad = jnp.zeros((batch, l_pad - l_valid, d), x.dtype)
    h0 = jnp.concatenate([meta, x, pad], axis=1).reshape(rows, d)
    xf, xb = _layer_norm(h0, emb_ln_g, emb_ln_b)

    pos = jnp.arange(l_pad, dtype=F32)
    half = RET_DK // 2
    inv = ROPE_BASE ** (-jnp.arange(half, dtype=F32) / half)
    ang = pos[:, None] * inv[None, :]
    cos, sin = jnp.cos(ang), jnp.sin(ang)
    log_gamma = jnp.log1p(-jnp.exp2(-5.0 - jnp.arange(ret_heads, dtype=F32)))

    lb_all = jnp.cumsum(jax.nn.softmax(hgrn_lb.astype(F32), axis=0), axis=0)
    lb_all = lb_all - lb_all[0:1]
    log_lb_all = jnp.log(lb_all)
    log_1m_lb_all = jnp.log1p(-lb_all)

    w_in_l = w_in[0:1].astype(BF16)

    for layer in range(depth):
        casts = [(w, layer) for w in (w_up, w_down, w_ret_out, w_hgrn_out, w_o)]
        casts += [(w_in, layer + 1)] if layer + 1 < depth else []
        proj, cast = _in_proj(xb, w_in_l, 0, bm, casts)
        w_up_l, w_down_l, w_ret_l, w_hgrn_l, w_o_l = cast[:5]
        w_in_l = cast[5] if layer + 1 < depth else None
        proj3 = proj.reshape(batch, l_pad, in_cols)
        ro = _retention(proj3, cos, sin, log_gamma, ret_heads).reshape(rows, ret_w)
        ho = _hgrn(proj3, 4 * ret_w, log_lb_all[layer:layer + 1], log_1m_lb_all[layer:layer + 1],
                   hgrn_norm_g[layer:layer + 1], hgrn_heads).reshape(rows, hgrn_w)
        merged = _merge(ro, ho, w_ret_l, w_hgrn_l, 0, proj, 4 * ret_w + 4 * hgrn_w,
                        b_gate[layer:layer + 1], bm)
        z1 = _matmul_resid(merged, w_o_l, 0, xf, alpha, bm)
        x1f, x1b = _layer_norm(z1, ln1_g[layer], ln1_b[layer])
        act = _ffn_up(x1b, w_up_l, 0, conv_w[layer], conv_b[layer:layer + 1], bm, l_pad)
        z2 = _matmul_resid(act, w_down_l, 0, x1f, alpha, bm)
        xf, xb = _layer_norm(z2, ln2_g[layer], ln2_b[layer])

    return xf.reshape(batch, l_pad, d)[:, n_meta:l_valid]
```

```python
import functools
import math

import numpy as np
import jax
import jax.numpy as jnp
from jax import lax
from jax.experimental import pallas as pl
from jax.experimental.pallas import tpu as pltpu

F32 = jnp.float32
BF16 = jnp.bfloat16

RET_DK = 256
RET_CHUNK = 128
HGRN_DK = 128
HGRN_BLOCK = 128
HGRN_LEVELS = (1, 2, 4, 8, 16, 32, 64)
CONV_WIDTH = 3
ROPE_BASE = 10000.0
LN_EPS = 1e-5
NORM_EPS = 1e-6
LOG2_E = 1.4426950408889634

V7X_VMEM_BYTES = 64 * 1024 * 1024
VMEM_LIMIT = V7X_VMEM_BYTES - 4 * 1024 * 1024
SUBLANES = 8
LANES = 128
MXU_COLS = 256
BF16_ROWS = 16
SEQ_ROWS_TARGET = 8192

NT = (((1,), (1,)), ((), ()))
TN = (((0,), (0,)), ((), ()))


def _params(n_axes):
    return pltpu.CompilerParams(dimension_semantics=("arbitrary",) * n_axes,
                                vmem_limit_bytes=VMEM_LIMIT)


def _pick_div(n, target, mult):
    best = 0
    for d in range(mult, min(n, target) + 1, mult):
        if n % d == 0:
            best = d
    assert best > 0, (n, target, mult)
    return best


def _widest_bn(n, vmem_bytes):
    return next(c for c in (1024, 512, MXU_COLS) if n % c == 0 and vmem_bytes(c) <= VMEM_LIMIT)


def _unroll(trips, choices=(3, 2, 1)):
    return next(u for u in choices if trips % u == 0)


def _sigmoid(x):
    return 0.5 + 0.5 * jnp.tanh(0.5 * x)


def _silu(x):
    h = 0.5 * x
    return h + h * jnp.tanh(h)


def _ln_kernel(x_ref, g_ref, b_ref, of_ref, ob_ref):
    x = x_ref[...]
    mu = jnp.mean(x, -1, keepdims=True)
    xc = x - mu
    var = jnp.mean(xc * xc, -1, keepdims=True)
    y = xc * lax.rsqrt(var + LN_EPS) * g_ref[...] + b_ref[...]
    of_ref[...] = y
    ob_ref[...] = y.astype(BF16)


def _layer_norm(x2d, g, b):
    rows, d = x2d.shape
    bm = _pick_div(rows, 528, BF16_ROWS)
    row_spec = pl.BlockSpec((bm, d), lambda i: (i, 0))
    vec_spec = pl.BlockSpec((1, d), lambda i: (0, 0))
    return pl.pallas_call(
        _ln_kernel,
        grid=(rows // bm,),
        in_specs=[row_spec, vec_spec, vec_spec],
        out_specs=[row_spec, row_spec],
        out_shape=[jax.ShapeDtypeStruct((rows, d), F32), jax.ShapeDtypeStruct((rows, d), BF16)],
        compiler_params=_params(1),
        name="layer_norm",
    )(x2d, g.reshape(1, d), b.reshape(1, d))


def _in_proj_kernel(a_ref, w_ref, *refs, n_cast):
    cast_in, o_ref, cast_out = refs[:n_cast], refs[n_cast], refs[n_cast + 1:]
    o_ref[...] = jnp.dot(a_ref[...], w_ref[...], preferred_element_type=F32).astype(o_ref.dtype)
    for src, dst in zip(cast_in, cast_out):
        dst[...] = src[...].astype(dst.dtype)


def _cast_rows(k, steps):
    return next(r for r in range(BF16_ROWS, k + 1, BF16_ROWS) if k % r == 0 and k // r <= steps)


def _in_proj(a, w_stack, layer, bm, casts):
    rows, k = a.shape
    n = w_stack.shape[-1]
    bn = _pick_div(n, 1024, LANES)
    nj = n // bn
    steps = (rows // bm) * nj
    cast_in_specs, cast_out_specs, cast_shapes = [], [], []
    for w, lw in casts:
        kw, nw = w.shape[1:]
        rb = _cast_rows(kw, steps)
        last = kw // rb - 1
        cast_in_specs.append(pl.BlockSpec(
            (None, rb, nw), lambda i, j, lw=lw, last=last: (lw, jnp.minimum(i * nj + j, last), 0)))
        cast_out_specs.append(pl.BlockSpec(
            (None, rb, nw), lambda i, j, last=last: (0, jnp.minimum(i * nj + j, last), 0)))
        cast_shapes.append(jax.ShapeDtypeStruct((1, kw, nw), BF16))
    outs = pl.pallas_call(
        functools.partial(_in_proj_kernel, n_cast=len(casts)),
        grid=(rows // bm, nj),
        in_specs=[pl.BlockSpec((bm, k), lambda i, j: (i, 0)),
                  pl.BlockSpec((None, k, bn), lambda i, j: (layer, 0, j))] + cast_in_specs,
        out_specs=[pl.BlockSpec((bm, bn), lambda i, j: (i, j))] + cast_out_specs,
        out_shape=[jax.ShapeDtypeStruct((rows, n), BF16)] + cast_shapes,
        compiler_params=_params(2),
        name="in_proj",
    )(a, w_stack, *[w for w, _ in casts])
    return outs[0], outs[1:]


def _mm_resid_kernel(a_ref, w_ref, r_ref, o_ref, *, alpha):
    o_ref[...] = alpha * r_ref[...] + jnp.dot(a_ref[...], w_ref[...], preferred_element_type=F32)


def _matmul_resid(a, w_stack, layer, resid, alpha, bm):
    rows, k = a.shape
    n = w_stack.shape[-1]
    bn = _widest_bn(n, lambda c: 4 * bm * k + 4 * k * c + 20 * bm * c)
    return pl.pallas_call(
        functools.partial(_mm_resid_kernel, alpha=alpha),
        grid=(rows // bm, n // bn),
        in_specs=[pl.BlockSpec((bm, k), lambda i, j: (i, 0)),
                  pl.BlockSpec((None, k, bn), lambda i, j: (layer, 0, j)),
                  pl.BlockSpec((bm, bn), lambda i, j: (i, j))],
        out_specs=pl.BlockSpec((bm, bn), lambda i, j: (i, j)),
        out_shape=jax.ShapeDtypeStruct((rows, n), F32),
        compiler_params=_params(2),
        name="proj_resid",
    )(a, w_stack, resid)


def _merge_kernel(ro_ref, ho_ref, wr_ref, wh_ref, gr_ref, gh_ref, bgr_ref, bgh_ref, o_ref):
    r = jnp.dot(ro_ref[...], wr_ref[...], preferred_element_type=F32)
    h = jnp.dot(ho_ref[...], wh_ref[...], preferred_element_type=F32)
    g_r = _sigmoid(gr_ref[...].astype(F32) + bgr_ref[...])
    g_h = _sigmoid(gh_ref[...].astype(F32) + bgh_ref[...])
    o_ref[...] = (g_r * r + g_h * h).astype(o_ref.dtype)


def _merge(ro, ho, wr_stack, wh_stack, layer, proj, gate_off, b_gate, bm):
    rows, rw = ro.shape
    hw = ho.shape[1]
    d = wr_stack.shape[-1]
    bn = _widest_bn(math.gcd(d, gate_off), lambda c: 4 * bm * (rw + hw) + 4 * (rw + hw) * c + 20 * bm * c)
    nj = d // bn
    goff = gate_off // bn
    return pl.pallas_call(
        _merge_kernel,
        grid=(rows // bm, nj),
        in_specs=[pl.BlockSpec((bm, rw), lambda i, j: (i, 0)),
                  pl.BlockSpec((bm, hw), lambda i, j: (i, 0)),
                  pl.BlockSpec((None, rw, bn), lambda i, j: (layer, 0, j)),
                  pl.BlockSpec((None, hw, bn), lambda i, j: (layer, 0, j)),
                  pl.BlockSpec((bm, bn), lambda i, j: (i, goff + j)),
                  pl.BlockSpec((bm, bn), lambda i, j: (i, goff + nj + j)),
                  pl.BlockSpec((1, bn), lambda i, j: (0, j)),
                  pl.BlockSpec((1, bn), lambda i, j: (0, nj + j))],
        out_specs=pl.BlockSpec((bm, bn), lambda i, j: (i, j)),
        out_shape=jax.ShapeDtypeStruct((rows, d), BF16),
        compiler_params=_params(2),
        name="gated_merge",
    )(ro, ho, wr_stack, wh_stack, proj, proj, b_gate, b_gate)


def _ffn_up_kernel(x_ref, wg_ref, wv_ref, cwg_ref, cwv_ref, cbg_ref, cbv_ref, o_ref,
                   ug_ref, uv_ref, *, bm, tiles_per_batch):
    i = pl.program_id(1)

    @pl.when(i % tiles_per_batch == 0)
    def _():
        ug_ref[0:SUBLANES, :] = jnp.zeros((SUBLANES, ug_ref.shape[1]), F32)
        uv_ref[0:SUBLANES, :] = jnp.zeros((SUBLANES, uv_ref.shape[1]), F32)

    x = x_ref[...]
    ug_ref[SUBLANES:SUBLANES + bm, :] = jnp.dot(x, wg_ref[...], preferred_element_type=F32)
    uv_ref[SUBLANES:SUBLANES + bm, :] = jnp.dot(x, wv_ref[...], preferred_element_type=F32)

    def conv(u_ref, cw_ref, cb_ref):
        u = u_ref[...]
        acc = cb_ref[...]
        for t in range(CONV_WIDTH):
            back = CONV_WIDTH - 1 - t
            shifted = u if back == 0 else pltpu.roll(u, back, axis=0)
            acc = acc + cw_ref[t:t + 1, :] * shifted[SUBLANES:SUBLANES + bm, :]
        return acc

    gate = conv(ug_ref, cwg_ref, cbg_ref)
    val = conv(uv_ref, cwv_ref, cbv_ref)
    o_ref[...] = (_silu(gate) * val).astype(o_ref.dtype)
    ug_ref[0:SUBLANES, :] = ug_ref[bm:bm + SUBLANES, :]
    uv_ref[0:SUBLANES, :] = uv_ref[bm:bm + SUBLANES, :]


def _ffn_up(x, w_stack, layer, conv_w, conv_b, bm, l_pad):
    rows, d = x.shape
    f2 = w_stack.shape[-1]
    f = f2 // 2
    bn = _pick_div(f, 512, MXU_COLS)
    nj = f // bn
    return pl.pallas_call(
        functools.partial(_ffn_up_kernel, bm=bm, tiles_per_batch=l_pad // bm),
        grid=(nj, rows // bm),
        in_specs=[pl.BlockSpec((bm, d), lambda j, i: (i, 0)),
                  pl.BlockSpec((None, d, bn), lambda j, i: (layer, 0, j)),
                  pl.BlockSpec((None, d, bn), lambda j, i: (layer, 0, nj + j)),
                  pl.BlockSpec((CONV_WIDTH, bn), lambda j, i: (0, j)),
                  pl.BlockSpec((CONV_WIDTH, bn), lambda j, i: (0, nj + j)),
                  pl.BlockSpec((1, bn), lambda j, i: (0, j)),
                  pl.BlockSpec((1, bn), lambda j, i: (0, nj + j))],
        out_specs=pl.BlockSpec((bm, bn), lambda j, i: (i, j)),
        out_shape=jax.ShapeDtypeStruct((rows, f), BF16),
        scratch_shapes=[pltpu.VMEM((bm + SUBLANES, bn), F32), pltpu.VMEM((bm + SUBLANES, bn), F32)],
        compiler_params=_params(2),
        name="ffn_up_conv_gate",
    )(x, w_stack, w_stack, conv_w, conv_w, conv_b, conv_b)


def _ret_kernel(lg_ref, q_ref, k_ref, v_ref, g_ref, cos_ref, sin_ref, o_ref, s_ref, *, n_chunks, heads):
    c = RET_CHUNK
    dk = RET_DK
    half = dk // 2
    scale = dk ** -0.5
    assert math.frexp(scale)[0] == 0.5, "key scale must be a power of two"

    @pl.when(pl.program_id(2) == 0)
    def _():
        s_ref[...] = jnp.zeros(s_ref.shape, F32)

    ii = lax.broadcasted_iota(jnp.int32, (c, c), 0)
    jj = lax.broadcasted_iota(jnp.int32, (c, c), 1)
    diff = (ii - jj).astype(F32)
    row = lax.broadcasted_iota(jnp.int32, (c, dk), 0).astype(F32)
    decay, w_q, w_k, chunk_decay = [], [], [], []
    for g in range(heads):
        lg = lg_ref[pl.program_id(1) * heads + g]
        decay.append(jnp.where(diff >= 0, jnp.exp(jnp.maximum(diff, 0.0) * lg) * scale, 0.0))
        w_q.append(jnp.exp((row + 1.0) * lg))
        w_k.append(jnp.exp((c - 1.0 - row) * lg) * scale)
        chunk_decay.append(jnp.exp(jnp.full((1, dk), c, F32) * lg))

    def rot(x, cos, sin):
        x1, x2 = x[:, :half], x[:, half:]
        return jnp.concatenate([x1 * cos - x2 * sin, x1 * sin + x2 * cos], axis=-1)

    def body(n, carry):
        r0 = pl.multiple_of(n * c, c)
        rows = pl.ds(r0, c)
        cos = cos_ref[rows, :]
        sin = sin_ref[rows, :]
        for g in range(heads):
            cols = slice(g * dk, (g + 1) * dk)
            q = rot(q_ref[rows, cols].astype(F32), cos, sin)
            k = rot(k_ref[rows, cols].astype(F32), cos, sin)
            v = v_ref[rows, cols].astype(BF16)
            state = s_ref[g]
            scores = lax.dot_general(q.astype(BF16), k.astype(BF16), NT,
                                     preferred_element_type=F32) * decay[g]
            o = jnp.dot(scores.astype(BF16), v, preferred_element_type=F32)
            o = o + jnp.dot((q * w_q[g]).astype(BF16), state.astype(BF16), preferred_element_type=F32)
            s_ref[g] = state * chunk_decay[g] + lax.dot_general(
                (k * w_k[g]).astype(BF16), v, TN, preferred_element_type=F32)
            mu = jnp.mean(o, -1, keepdims=True)
            oc = o - mu
            var = jnp.mean(oc * oc, -1, keepdims=True)
            y = oc * lax.rsqrt(var + LN_EPS) * _silu(g_ref[rows, cols].astype(F32))
            o_ref[rows, cols] = y.astype(o_ref.dtype)
        return carry

    lax.fori_loop(0, n_chunks, body, 0, unroll=_unroll(n_chunks, (11, 3, 2, 1)))


def _retention(proj3, cos, sin, log_gamma, n_heads):
    b, l_pad, _ = proj3.shape
    heads = 2 if n_heads % 2 == 0 else 1
    groups = n_heads // heads
    width = heads * RET_DK
    rb = _pick_div(l_pad, SEQ_ROWS_TARGET, RET_CHUNK)

    def col(section):
        return pl.BlockSpec((None, rb, width), lambda bi, hg, r: (bi, r, section * groups + hg))

    tab = pl.BlockSpec((rb, RET_DK // 2), lambda bi, hg, r: (r, 0))
    return pl.pallas_call(
        functools.partial(_ret_kernel, n_chunks=rb // RET_CHUNK, heads=heads),
        grid=(b, groups, l_pad // rb),
        in_specs=[pl.BlockSpec(memory_space=pltpu.SMEM), col(0), col(1), col(2), col(3), tab, tab],
        out_specs=pl.BlockSpec((None, rb, width), lambda bi, hg, r: (bi, r, hg)),
        out_shape=jax.ShapeDtypeStruct((b, l_pad, n_heads * RET_DK), BF16),
        scratch_shapes=[pltpu.VMEM((heads, RET_DK, RET_DK), F32)],
        compiler_params=_params(3),
        name="retention",
    )(log_gamma, proj3, proj3, proj3, proj3, cos, sin)


def _hgrn_constants():
    t = HGRN_BLOCK
    idx = np.arange(t)
    sums, upper, pair = [], [], []
    for s in HGRN_LEVELS:
        r = (idx // (2 * s)) * 2 * s + s - 1
        up = (idx % (2 * s)) >= s
        tt = idx[None, :]
        if s < SUBLANES:
            sums.append(np.where(up[:, None], (tt > r[:, None]) & (tt <= idx[:, None]),
                                 (tt > idx[:, None]) & (tt <= r[:, None])))
        upper.append(np.broadcast_to(up[:, None], (t, HGRN_DK)))
        same = (idx[:, None] // (2 * s)) == (idx[None, :] // (2 * s))
        pair.append(same & up[:, None] & ~up[None, :])
    sums.append(idx[None, :] <= idx[:, None])
    pair.append(np.eye(t, dtype=bool))
    return (np.tile(np.concatenate(sums, 0), (1, 2)).astype(np.float32), np.stack(upper).astype(np.float32),
            np.tile(np.stack(pair), (1, 1, 2)).astype(np.float32))


def _hgrn_kernel(q_ref, f_ref, v_ref, g_ref, loglb_ref, l1mlb_ref, ng_ref, sums_ref, upper_ref,
                 pair_ref, o_ref, st_ref, *, n_blocks, heads):
    t = HGRN_BLOCK
    dk = HGRN_DK
    nl = len(HGRN_LEVELS)
    n_mxu = sum(1 for s in HGRN_LEVELS if s < SUBLANES)

    @pl.when(pl.program_id(2) == 0)
    def _():
        st_ref[...] = jnp.zeros(st_ref.shape, F32)

    log_lb = loglb_ref[...]
    log_1m_lb = l1mlb_ref[...]
    norm_g = ng_ref[...]

    def prepare(n):
        rows = pl.ds(pl.multiple_of(n * t, t), t)
        hf = f_ref[rows, :].astype(F32)
        log_sig = jnp.minimum(hf, 0.0) - jnp.log(1.0 + jnp.exp(-jnp.abs(hf)))
        b2 = log_1m_lb + log_sig
        logf = jnp.maximum(log_lb, b2) + jnp.log(1.0 + jnp.exp(-jnp.abs(log_lb - b2)))
        k_all = jnp.exp(b2 - hf)
        q_all = _silu(q_ref[rows, :].astype(F32))

        logf = logf * LOG2_E
        f_hi = logf.astype(BF16)
        f_lo = (logf - f_hi.astype(F32)).astype(BF16)
        e_all = jnp.dot(sums_ref[...], jnp.concatenate([f_hi, f_lo], axis=0),
                        preferred_element_type=F32)

        z_out = [[None] * heads for _ in range(nl)]
        qk_out = [[None] * heads for _ in range(4)]
        dec_out = [None] * heads
        for g in range(heads):
            cols = slice(g * dk, (g + 1) * dk)
            q, k = q_all[:, cols], k_all[:, cols]
            prefix = e_all[n_mxu * t:(n_mxu + 1) * t, cols]
            for lvl, s in enumerate(HGRN_LEVELS):
                if lvl < n_mxu:
                    ex = e_all[lvl * t:(lvl + 1) * t, cols]
                    src = jnp.where(upper_ref[lvl] > 0.0, q, k)
                else:
                    ex_parts, src_parts = [], []
                    for r0 in range(0, t, 2 * s):
                        lo, hi = slice(r0, r0 + s), slice(r0 + s, r0 + 2 * s)
                        bound = prefix[r0 + s - 1:r0 + s, :]
                        ex_parts += [bound - prefix[lo, :], prefix[hi, :] - bound]
                        src_parts += [k[lo, :], q[hi, :]]
                    ex = jnp.concatenate(ex_parts, axis=0)
                    src = jnp.concatenate(src_parts, axis=0)
                z_out[lvl][g] = (src * jnp.exp2(ex)).astype(BF16)

            total = prefix[t - 1:t, :]
            qk_out[0][g] = q.astype(BF16)
            qk_out[1][g] = k.astype(BF16)
            qk_out[2][g] = (q * jnp.exp2(prefix)).astype(BF16)
            qk_out[3][g] = (k * jnp.exp2(total - prefix)).astype(BF16)
            dec_out[g] = jnp.exp2(total)

        def cat(parts):
            return jnp.concatenate(parts, axis=1)

        return (tuple(cat(zl) for zl in z_out), tuple(cat(x) for x in qk_out), cat(dec_out))

    zero_tile = jnp.zeros((t, dk), BF16)

    def block_diag(x):
        return jnp.concatenate([jnp.concatenate([x[:, :dk], zero_tile], axis=1),
                                jnp.concatenate([zero_tile, x[:, dk:]], axis=1)], axis=0)

    def consume(n, prepared):
        z_all, qk_all, dec_all = prepared
        rows = pl.ds(pl.multiple_of(n * t, t), t)
        for p in range(heads // 2):
            cols2 = slice(2 * p * dk, 2 * (p + 1) * dk)
            a = pair_ref[nl] * lax.dot_general(qk_all[0][:, cols2], block_diag(qk_all[1][:, cols2]), NT,
                                               preferred_element_type=F32).astype(BF16)
            for lvl in range(nl):
                z = z_all[lvl][:, cols2]
                a = a + pair_ref[lvl] * lax.dot_general(z, block_diag(z), NT,
                                                        preferred_element_type=F32).astype(BF16)
            v = v_ref[rows, cols2].astype(BF16)
            st = [st_ref[2 * p], st_ref[2 * p + 1]]
            st_pair = jnp.concatenate(st, axis=1).astype(BF16)
            o = jnp.dot(a, block_diag(v), preferred_element_type=F32)
            o = o + lax.dot_general(qk_all[2][:, cols2], block_diag(st_pair), NT, preferred_element_type=F32)
            for h in range(2):
                cols = slice((2 * p + h) * dk, (2 * p + h + 1) * dk)
                half = slice(h * dk, (h + 1) * dk)
                st_ref[2 * p + h] = st[h] * dec_all[:, cols] + lax.dot_general(
                    v[:, half], qk_all[3][:, cols], TN, preferred_element_type=F32)
                oh = o[:, half]
                y = oh * lax.rsqrt(jnp.mean(oh * oh, -1, keepdims=True) + NORM_EPS)
                gate = norm_g[:, cols] * _silu(g_ref[rows, cols].astype(F32))
                o_ref[rows, cols] = (y * gate).astype(o_ref.dtype)

    def body(n, prepared):
        consume(n, prepared)
        return prepare(jnp.minimum(n + 1, n_blocks - 1))

    lax.fori_loop(0, n_blocks, body, prepare(0), unroll=_unroll(n_blocks, (11, 3, 2, 1)))


def _hgrn(proj3, col_off, log_lb, log_1m_lb, norm_g, n_heads):
    b, l_pad, _ = proj3.shape
    sums, upper, pair = _hgrn_constants()
    heads = next(h for h in (4, 2) if n_heads % h == 0)
    groups = n_heads // heads
    width = heads * HGRN_DK
    base = col_off // width
    rb = _pick_div(l_pad, SEQ_ROWS_TARGET, HGRN_BLOCK)

    def col(section):
        return pl.BlockSpec((None, rb, width), lambda bi, hg, r: (bi, r, base + section * groups + hg))

    vec = pl.BlockSpec((1, width), lambda bi, hg, r: (0, hg))

    def const(shape):
        return pl.BlockSpec(shape, lambda bi, hg, r: (0,) * len(shape))

    return pl.pallas_call(
        functools.partial(_hgrn_kernel, n_blocks=rb // HGRN_BLOCK, heads=heads),
        grid=(b, groups, l_pad // rb),
        in_specs=[col(0), col(1), col(2), col(3), vec, vec, vec,
                  const(sums.shape), const(upper.shape), const(pair.shape)],
        out_specs=pl.BlockSpec((None, rb, width), lambda bi, hg, r: (bi, r, hg)),
        out_shape=jax.ShapeDtypeStruct((b, l_pad, n_heads * HGRN_DK), BF16),
        scratch_shapes=[pltpu.VMEM((heads, HGRN_DK, HGRN_DK), F32)],
        compiler_params=_params(3),
        name="hgrn2",
    )(proj3, proj3, proj3, proj3, log_lb, log_1m_lb, norm_g,
      jnp.asarray(sums, BF16), jnp.asarray(upper), jnp.asarray(pair, BF16))


def kernel(x, meta_tokens, emb_ln_g, emb_ln_b, hgrn_lb, w_in, b_gate, w_ret_out, w_hgrn_out, w_o,
           hgrn_norm_g, ln1_g, ln1_b, w_up, conv_w, conv_b, w_down, ln2_g, ln2_b):
    batch, seq, d = x.shape
    depth = w_in.shape[0]
    n_meta = meta_tokens.shape[0]
    l_valid = n_meta + seq
    l_pad = -(-l_valid // LANES) * LANES
    rows = batch * l_pad
    ret_w = w_ret_out.shape[1]
    hgrn_w = w_hgrn_out.shape[1]
    ret_heads = ret_w // RET_DK
    hgrn_heads = hgrn_w // HGRN_DK
    in_cols = w_in.shape[-1]
    assert in_cols == 4 * ret_w + 4 * hgrn_w + 2 * d
    alpha = float((2 * depth) ** 0.25)
    bm = _pick_div(l_pad, 1056, BF16_ROWS)

    meta = jnp.broadcast_to(meta_tokens[None].astype(x.dtype), (batch, n_meta, d))
    pad = jnp.zeros((batch, l_pad - l_valid, d), x.dtype)
    h0 = jnp.concatenate([meta, x, pad], axis=1).reshape(rows, d)
    xf, xb = _layer_norm(h0, emb_ln_g, emb_ln_b)

    pos = jnp.arange(l_pad, dtype=F32)
    half = RET_DK // 2
    inv = ROPE_BASE ** (-jnp.arange(half, dtype=F32) / half)
    ang = pos[:, None] * inv[None, :]
    cos, sin = jnp.cos(ang), jnp.sin(ang)
    log_gamma = jnp.log1p(-jnp.exp2(-5.0 - jnp.arange(ret_heads, dtype=F32)))

    lb_all = jnp.cumsum(jax.nn.softmax(hgrn_lb.astype(F32), axis=0), axis=0)
    lb_all = lb_all - lb_all[0:1]
    log_lb_all = jnp.log(lb_all)
    log_1m_lb_all = jnp.log1p(-lb_all)

    w_in_l = w_in[0:1].astype(BF16)

    for layer in range(depth):
        casts = [(w, layer) for w in (w_up, w_down, w_ret_out, w_hgrn_out, w_o)]
        casts += [(w_in, layer + 1)] if layer + 1 < depth else []
        proj, cast = _in_proj(xb, w_in_l, 0, bm, casts)
        w_up_l, w_down_l, w_ret_l, w_hgrn_l, w_o_l = cast[:5]
        w_in_l = cast[5] if layer + 1 < depth else None
        proj3 = proj.reshape(batch, l_pad, in_cols)
        ro = _retention(proj3, cos, sin, log_gamma, ret_heads).reshape(rows, ret_w)
        ho = _hgrn(proj3, 4 * ret_w, log_lb_all[layer:layer + 1], log_1m_lb_all[layer:layer + 1],
                   hgrn_norm_g[layer:layer + 1], hgrn_heads).reshape(rows, hgrn_w)
        merged = _merge(ro, ho, w_ret_l, w_hgrn_l, 0, proj, 4 * ret_w + 4 * hgrn_w,
                        b_gate[layer:layer + 1], bm)
        z1 = _matmul_resid(merged, w_o_l, 0, xf, alpha, bm)
        x1f, x1b = _layer_norm(z1, ln1_g[layer], ln1_b[layer])
        act = _ffn_up(x1b, w_up_l, 0, conv_w[layer], conv_b[layer:layer + 1], bm, l_pad)
        z2 = _matmul_resid(act, w_down_l, 0, x1f, alpha, bm)
        xf, xb = _layer_norm(z2, ln2_g[layer], ln2_b[layer])

    return xf.reshape(batch, l_pad, d)[:, n_meta:l_valid]
```

```python
import functools
import math

import numpy as np
import jax
import jax.numpy as jnp
from jax import lax
from jax.experimental import pallas as pl
from jax.experimental.pallas import tpu as pltpu

F32 = jnp.float32
BF16 = jnp.bfloat16

RET_DK = 256
RET_CHUNK = 128
HGRN_DK = 128
HGRN_BLOCK = 128
HGRN_LEVELS = (1, 2, 4, 8, 16, 32, 64)
HGRN_MXU_LEVEL_LIMIT = 4
CONV_WIDTH = 3
ROPE_BASE = 10000.0
LN_EPS = 1e-5
NORM_EPS = 1e-6
LOG2_E = 1.4426950408889634

V7X_VMEM_BYTES = 64 * 1024 * 1024
VMEM_LIMIT = V7X_VMEM_BYTES - 4 * 1024 * 1024
SUBLANES = 8
LANES = 128
MXU_COLS = 256
BF16_ROWS = 16
SEQ_ROWS_TARGET = 8192

NT = (((1,), (1,)), ((), ()))
TN = (((0,), (0,)), ((), ()))


def _params(n_axes):
    return pltpu.CompilerParams(dimension_semantics=("arbitrary",) * n_axes,
                                vmem_limit_bytes=VMEM_LIMIT)


def _pick_div(n, target, mult):
    best = 0
    for d in range(mult, min(n, target) + 1, mult):
        if n % d == 0:
            best = d
    assert best > 0, (n, target, mult)
    return best


def _widest_bn(n, vmem_bytes):
    return next(c for c in (1024, 512, MXU_COLS) if n % c == 0 and vmem_bytes(c) <= VMEM_LIMIT)


def _unroll(trips, choices=(3, 2, 1)):
    return next(u for u in choices if trips % u == 0)


def _sigmoid(x):
    return 0.5 + 0.5 * jnp.tanh(0.5 * x)


def _silu(x):
    h = 0.5 * x
    return h + h * jnp.tanh(h)


def _ln_kernel(x_ref, g_ref, b_ref, of_ref, ob_ref):
    x = x_ref[...]
    mu = jnp.mean(x, -1, keepdims=True)
    xc = x - mu
    var = jnp.mean(xc * xc, -1, keepdims=True)
    y = xc * lax.rsqrt(var + LN_EPS) * g_ref[...] + b_ref[...]
    of_ref[...] = y
    ob_ref[...] = y.astype(BF16)


def _layer_norm(x2d, g, b):
    rows, d = x2d.shape
    bm = _pick_div(rows, 528, BF16_ROWS)
    row_spec = pl.BlockSpec((bm, d), lambda i: (i, 0))
    vec_spec = pl.BlockSpec((1, d), lambda i: (0, 0))
    return pl.pallas_call(
        _ln_kernel,
        grid=(rows // bm,),
        in_specs=[row_spec, vec_spec, vec_spec],
        out_specs=[row_spec, row_spec],
        out_shape=[jax.ShapeDtypeStruct((rows, d), F32), jax.ShapeDtypeStruct((rows, d), BF16)],
        compiler_params=_params(1),
        name="layer_norm",
    )(x2d, g.reshape(1, d), b.reshape(1, d))


def _in_proj_kernel(a_ref, w_ref, *refs, n_cast):
    cast_in, o_ref, cast_out = refs[:n_cast], refs[n_cast], refs[n_cast + 1:]
    o_ref[...] = jnp.dot(a_ref[...], w_ref[...], preferred_element_type=F32).astype(o_ref.dtype)
    for src, dst in zip(cast_in, cast_out):
        dst[...] = src[...].astype(dst.dtype)


def _cast_rows(k, steps):
    return next(r for r in range(BF16_ROWS, k + 1, BF16_ROWS) if k % r == 0 and k // r <= steps)


def _in_proj(a, w_stack, layer, bm, casts):
    rows, k = a.shape
    n = w_stack.shape[-1]
    bn = _pick_div(n, 1024, LANES)
    nj = n // bn
    steps = (rows // bm) * nj
    cast_in_specs, cast_out_specs, cast_shapes = [], [], []
    for w, lw in casts:
        kw, nw = w.shape[1:]
        rb = _cast_rows(kw, steps)
        last = kw // rb - 1
        cast_in_specs.append(pl.BlockSpec(
            (None, rb, nw), lambda i, j, lw=lw, last=last: (lw, jnp.minimum(i * nj + j, last), 0)))
        cast_out_specs.append(pl.BlockSpec(
            (None, rb, nw), lambda i, j, last=last: (0, jnp.minimum(i * nj + j, last), 0)))
        cast_shapes.append(jax.ShapeDtypeStruct((1, kw, nw), BF16))
    outs = pl.pallas_call(
        functools.partial(_in_proj_kernel, n_cast=len(casts)),
        grid=(rows // bm, nj),
        in_specs=[pl.BlockSpec((bm, k), lambda i, j: (i, 0)),
                  pl.BlockSpec((None, k, bn), lambda i, j: (layer, 0, j))] + cast_in_specs,
        out_specs=[pl.BlockSpec((bm, bn), lambda i, j: (i, j))] + cast_out_specs,
        out_shape=[jax.ShapeDtypeStruct((rows, n), BF16)] + cast_shapes,
        compiler_params=_params(2),
        name="in_proj",
    )(a, w_stack, *[w for w, _ in casts])
    return outs[0], outs[1:]


def _mm_resid_kernel(a_ref, w_ref, r_ref, o_ref, *, alpha):
    o_ref[...] = alpha * r_ref[...] + jnp.dot(a_ref[...], w_ref[...], preferred_element_type=F32)


def _matmul_resid(a, w_stack, layer, resid, alpha, bm):
    rows, k = a.shape
    n = w_stack.shape[-1]
    bn = _widest_bn(n, lambda c: 4 * bm * k + 4 * k * c + 20 * bm * c)
    return pl.pallas_call(
        functools.partial(_mm_resid_kernel, alpha=alpha),
        grid=(rows // bm, n // bn),
        in_specs=[pl.BlockSpec((bm, k), lambda i, j: (i, 0)),
                  pl.BlockSpec((None, k, bn), lambda i, j: (layer, 0, j)),
                  pl.BlockSpec((bm, bn), lambda i, j: (i, j))],
        out_specs=pl.BlockSpec((bm, bn), lambda i, j: (i, j)),
        out_shape=jax.ShapeDtypeStruct((rows, n), F32),
        compiler_params=_params(2),
        name="proj_resid",
    )(a, w_stack, resid)


def _merge_kernel(ro_ref, ho_ref, wr_ref, wh_ref, gr_ref, gh_ref, bgr_ref, bgh_ref, o_ref):
    r = jnp.dot(ro_ref[...], wr_ref[...], preferred_element_type=F32)
    h = jnp.dot(ho_ref[...], wh_ref[...], preferred_element_type=F32)
    g_r = _sigmoid(gr_ref[...].astype(F32) + bgr_ref[...])
    g_h = _sigmoid(gh_ref[...].astype(F32) + bgh_ref[...])
    o_ref[...] = (g_r * r + g_h * h).astype(o_ref.dtype)


def _merge(ro, ho, wr_stack, wh_stack, layer, proj, gate_off, b_gate, bm):
    rows, rw = ro.shape
    hw = ho.shape[1]
    d = wr_stack.shape[-1]
    bn = _widest_bn(math.gcd(d, gate_off), lambda c: 4 * bm * (rw + hw) + 4 * (rw + hw) * c + 20 * bm * c)
    nj = d // bn
    goff = gate_off // bn
    return pl.pallas_call(
        _merge_kernel,
        grid=(rows // bm, nj),
        in_specs=[pl.BlockSpec((bm, rw), lambda i, j: (i, 0)),
                  pl.BlockSpec((bm, hw), lambda i, j: (i, 0)),
                  pl.BlockSpec((None, rw, bn), lambda i, j: (layer, 0, j)),
                  pl.BlockSpec((None, hw, bn), lambda i, j: (layer, 0, j)),
                  pl.BlockSpec((bm, bn), lambda i, j: (i, goff + j)),
                  pl.BlockSpec((bm, bn), lambda i, j: (i, goff + nj + j)),
                  pl.BlockSpec((1, bn), lambda i, j: (0, j)),
                  pl.BlockSpec((1, bn), lambda i, j: (0, nj + j))],
        out_specs=pl.BlockSpec((bm, bn), lambda i, j: (i, j)),
        out_shape=jax.ShapeDtypeStruct((rows, d), BF16),
        compiler_params=_params(2),
        name="gated_merge",
    )(ro, ho, wr_stack, wh_stack, proj, proj, b_gate, b_gate)


def _ffn_up_kernel(x_ref, wg_ref, wv_ref, cwg_ref, cwv_ref, cbg_ref, cbv_ref, o_ref,
                   ug_ref, uv_ref, *, bm, tiles_per_batch):
    i = pl.program_id(1)

    @pl.when(i % tiles_per_batch == 0)
    def _():
        ug_ref[0:SUBLANES, :] = jnp.zeros((SUBLANES, ug_ref.shape[1]), F32)
        uv_ref[0:SUBLANES, :] = jnp.zeros((SUBLANES, uv_ref.shape[1]), F32)

    x = x_ref[...]
    ug_ref[SUBLANES:SUBLANES + bm, :] = jnp.dot(x, wg_ref[...], preferred_element_type=F32)
    uv_ref[SUBLANES:SUBLANES + bm, :] = jnp.dot(x, wv_ref[...], preferred_element_type=F32)

    def conv(u_ref, cw_ref, cb_ref):
        u = u_ref[...]
        acc = cb_ref[...]
        for t in range(CONV_WIDTH):
            back = CONV_WIDTH - 1 - t
            shifted = u if back == 0 else pltpu.roll(u, back, axis=0)
            acc = acc + cw_ref[t:t + 1, :] * shifted[SUBLANES:SUBLANES + bm, :]
        return acc

    gate = conv(ug_ref, cwg_ref, cbg_ref)
    val = conv(uv_ref, cwv_ref, cbv_ref)
    o_ref[...] = (_silu(gate) * val).astype(o_ref.dtype)
    ug_ref[0:SUBLANES, :] = ug_ref[bm:bm + SUBLANES, :]
    uv_ref[0:SUBLANES, :] = uv_ref[bm:bm + SUBLANES, :]


def _ffn_up(x, w_stack, layer, conv_w, conv_b, bm, l_pad):
    rows, d = x.shape
    f2 = w_stack.shape[-1]
    f = f2 // 2
    bn = _pick_div(f, 512, MXU_COLS)
    nj = f // bn
    return pl.pallas_call(
        functools.partial(_ffn_up_kernel, bm=bm, tiles_per_batch=l_pad // bm),
        grid=(nj, rows // bm),
        in_specs=[pl.BlockSpec((bm, d), lambda j, i: (i, 0)),
                  pl.BlockSpec((None, d, bn), lambda j, i: (layer, 0, j)),
                  pl.BlockSpec((None, d, bn), lambda j, i: (layer, 0, nj + j)),
                  pl.BlockSpec((CONV_WIDTH, bn), lambda j, i: (0, j)),
                  pl.BlockSpec((CONV_WIDTH, bn), lambda j, i: (0, nj + j)),
                  pl.BlockSpec((1, bn), lambda j, i: (0, j)),
                  pl.BlockSpec((1, bn), lambda j, i: (0, nj + j))],
        out_specs=pl.BlockSpec((bm, bn), lambda j, i: (i, j)),
        out_shape=jax.ShapeDtypeStruct((rows, f), BF16),
        scratch_shapes=[pltpu.VMEM((bm + SUBLANES, bn), F32), pltpu.VMEM((bm + SUBLANES, bn), F32)],
        compiler_params=_params(2),
        name="ffn_up_conv_gate",
    )(x, w_stack, w_stack, conv_w, conv_w, conv_b, conv_b)


def _ret_kernel(lg_ref, q_ref, k_ref, v_ref, g_ref, cos_ref, sin_ref, o_ref, s_ref, *, n_chunks, heads):
    c = RET_CHUNK
    dk = RET_DK
    half = dk // 2
    scale = dk ** -0.5
    assert math.frexp(scale)[0] == 0.5, "key scale must be a power of two"

    @pl.when(pl.program_id(2) == 0)
    def _():
        s_ref[...] = jnp.zeros(s_ref.shape, F32)

    ii = lax.broadcasted_iota(jnp.int32, (c, c), 0)
    jj = lax.broadcasted_iota(jnp.int32, (c, c), 1)
    diff = (ii - jj).astype(F32)
    row = lax.broadcasted_iota(jnp.int32, (c, dk), 0).astype(F32)
    decay, w_q, w_k, chunk_decay = [], [], [], []
    for g in range(heads):
        lg = lg_ref[pl.program_id(1) * heads + g]
        decay.append(jnp.where(diff >= 0, jnp.exp(jnp.maximum(diff, 0.0) * lg) * scale, 0.0))
        w_q.append(jnp.exp((row + 1.0) * lg))
        w_k.append(jnp.exp((c - 1.0 - row) * lg) * scale)
        chunk_decay.append(jnp.exp(jnp.full((1, dk), c, F32) * lg))

    def rot(x, cos, sin):
        x1, x2 = x[:, :half], x[:, half:]
        return jnp.concatenate([x1 * cos - x2 * sin, x1 * sin + x2 * cos], axis=-1)

    def body(n, carry):
        r0 = pl.multiple_of(n * c, c)
        rows = pl.ds(r0, c)
        cos = cos_ref[rows, :]
        sin = sin_ref[rows, :]
        for g in range(heads):
            cols = slice(g * dk, (g + 1) * dk)
            q = rot(q_ref[rows, cols].astype(F32), cos, sin)
            k = rot(k_ref[rows, cols].astype(F32), cos, sin)
            v = v_ref[rows, cols].astype(BF16)
            state = s_ref[g]
            scores = lax.dot_general(q.astype(BF16), k.astype(BF16), NT,
                                     preferred_element_type=F32) * decay[g]
            o = jnp.dot(scores.astype(BF16), v, preferred_element_type=F32)
            o = o + jnp.dot((q * w_q[g]).astype(BF16), state.astype(BF16), preferred_element_type=F32)
            s_ref[g] = state * chunk_decay[g] + lax.dot_general(
                (k * w_k[g]).astype(BF16), v, TN, preferred_element_type=F32)
            mu = jnp.mean(o, -1, keepdims=True)
            oc = o - mu
            var = jnp.mean(oc * oc, -1, keepdims=True)
            y = oc * lax.rsqrt(var + LN_EPS) * _silu(g_ref[rows, cols].astype(F32))
            o_ref[rows, cols] = y.astype(o_ref.dtype)
        return carry

    lax.fori_loop(0, n_chunks, body, 0, unroll=_unroll(n_chunks, (11, 3, 2, 1)))


def _retention(proj3, cos, sin, log_gamma, n_heads):
    b, l_pad, _ = proj3.shape
    heads = 2 if n_heads % 2 == 0 else 1
    groups = n_heads // heads
    width = heads * RET_DK
    rb = _pick_div(l_pad, SEQ_ROWS_TARGET, RET_CHUNK)

    def col(section):
        return pl.BlockSpec((None, rb, width), lambda bi, hg, r: (bi, r, section * groups + hg))

    tab = pl.BlockSpec((rb, RET_DK // 2), lambda bi, hg, r: (r, 0))
    return pl.pallas_call(
        functools.partial(_ret_kernel, n_chunks=rb // RET_CHUNK, heads=heads),
        grid=(b, groups, l_pad // rb),
        in_specs=[pl.BlockSpec(memory_space=pltpu.SMEM), col(0), col(1), col(2), col(3), tab, tab],
        out_specs=pl.BlockSpec((None, rb, width), lambda bi, hg, r: (bi, r, hg)),
        out_shape=jax.ShapeDtypeStruct((b, l_pad, n_heads * RET_DK), BF16),
        scratch_shapes=[pltpu.VMEM((heads, RET_DK, RET_DK), F32)],
        compiler_params=_params(3),
        name="retention",
    )(log_gamma, proj3, proj3, proj3, proj3, cos, sin)


def _hgrn_constants():
    t = HGRN_BLOCK
    idx = np.arange(t)
    sums, upper, pair = [], [], []
    for s in HGRN_LEVELS:
        r = (idx // (2 * s)) * 2 * s + s - 1
        up = (idx % (2 * s)) >= s
        tt = idx[None, :]
        if s < HGRN_MXU_LEVEL_LIMIT:
            sums.append(np.where(up[:, None], (tt > r[:, None]) & (tt <= idx[:, None]),
                                 (tt > idx[:, None]) & (tt <= r[:, None])))
        upper.append(np.broadcast_to(up[:, None], (t, HGRN_DK)))
        same = (idx[:, None] // (2 * s)) == (idx[None, :] // (2 * s))
        pair.append(same & up[:, None] & ~up[None, :])
    sums.append(idx[None, :] <= idx[:, None])
    pair.append(np.eye(t, dtype=bool))
    return (np.tile(np.concatenate(sums, 0), (1, 2)).astype(np.float32), np.stack(upper).astype(np.float32),
            np.tile(np.stack(pair), (1, 1, 2)).astype(np.float32))


def _hgrn_kernel(q_ref, f_ref, v_ref, g_ref, loglb_ref, l1mlb_ref, ng_ref, sums_ref, upper_ref,
                 pair_ref, o_ref, st_ref, *, n_blocks, heads):
    t = HGRN_BLOCK
    dk = HGRN_DK
    nl = len(HGRN_LEVELS)
    n_mxu = sum(1 for s in HGRN_LEVELS if s < HGRN_MXU_LEVEL_LIMIT)

    @pl.when(pl.program_id(2) == 0)
    def _():
        st_ref[...] = jnp.zeros(st_ref.shape, F32)

    log_lb = loglb_ref[...]
    log_1m_lb = l1mlb_ref[...]
    norm_g = ng_ref[...]

    def prepare(n):
        rows = pl.ds(pl.multiple_of(n * t, t), t)
        hf = f_ref[rows, :].astype(F32)
        log_sig = jnp.minimum(hf, 0.0) - jnp.log(1.0 + jnp.exp(-jnp.abs(hf)))
        b2 = log_1m_lb + log_sig
        logf = jnp.maximum(log_lb, b2) + jnp.log(1.0 + jnp.exp(-jnp.abs(log_lb - b2)))
        k_all = jnp.exp(b2 - hf)
        q_all = _silu(q_ref[rows, :].astype(F32))

        logf = logf * LOG2_E
        f_hi = logf.astype(BF16)
        f_lo = (logf - f_hi.astype(F32)).astype(BF16)
        e_all = jnp.dot(sums_ref[...], jnp.concatenate([f_hi, f_lo], axis=0),
                        preferred_element_type=F32)

        z_out = [[None] * heads for _ in range(nl)]
        qk_out = [[None] * heads for _ in range(4)]
        dec_out = [None] * heads
        for g in range(heads):
            cols = slice(g * dk, (g + 1) * dk)
            q, k = q_all[:, cols], k_all[:, cols]
            prefix = e_all[n_mxu * t:(n_mxu + 1) * t, cols]
            for lvl, s in enumerate(HGRN_LEVELS):
                if lvl < n_mxu:
                    ex = e_all[lvl * t:(lvl + 1) * t, cols]
                    src = jnp.where(upper_ref[lvl] > 0.0, q, k)
                else:
                    ex_parts, src_parts = [], []
                    for r0 in range(0, t, 2 * s):
                        lo, hi = slice(r0, r0 + s), slice(r0 + s, r0 + 2 * s)
                        bound = prefix[r0 + s - 1:r0 + s, :]
                        ex_parts += [bound - prefix[lo, :], prefix[hi, :] - bound]
                        src_parts += [k[lo, :], q[hi, :]]
                    ex = jnp.concatenate(ex_parts, axis=0)
                    src = jnp.concatenate(src_parts, axis=0)
                z_out[lvl][g] = (src * jnp.exp2(ex)).astype(BF16)

            total = prefix[t - 1:t, :]
            qk_out[0][g] = q.astype(BF16)
            qk_out[1][g] = k.astype(BF16)
            qk_out[2][g] = (q * jnp.exp2(prefix)).astype(BF16)
            qk_out[3][g] = (k * jnp.exp2(total - prefix)).astype(BF16)
            dec_out[g] = jnp.exp2(total)

        def cat(parts):
            return jnp.concatenate(parts, axis=1)

        return (tuple(cat(zl) for zl in z_out), tuple(cat(x) for x in qk_out), cat(dec_out))

    zero_tile = jnp.zeros((t, dk), BF16)

    def block_diag(x):
        return jnp.concatenate([jnp.concatenate([x[:, :dk], zero_tile], axis=1),
                                jnp.concatenate([zero_tile, x[:, dk:]], axis=1)], axis=0)

    def consume(n, prepared):
        z_all, qk_all, dec_all = prepared
        rows = pl.ds(pl.multiple_of(n * t, t), t)
        for p in range(heads // 2):
            cols2 = slice(2 * p * dk, 2 * (p + 1) * dk)
            a = pair_ref[nl] * lax.dot_general(qk_all[0][:, cols2], block_diag(qk_all[1][:, cols2]), NT,
                                               preferred_element_type=F32).astype(BF16)
            for lvl in range(nl):
                z = z_all[lvl][:, cols2]
                a = a + pair_ref[lvl] * lax.dot_general(z, block_diag(z), NT,
                                                        preferred_element_type=F32).astype(BF16)
            v = v_ref[rows, cols2].astype(BF16)
            st = [st_ref[2 * p], st_ref[2 * p + 1]]
            st_pair = jnp.concatenate(st, axis=1).astype(BF16)
            o = jnp.dot(a, block_diag(v), preferred_element_type=F32)
            o = o + lax.dot_general(qk_all[2][:, cols2], block_diag(st_pair), NT, preferred_element_type=F32)
            for h in range(2):
                cols = slice((2 * p + h) * dk, (2 * p + h + 1) * dk)
                half = slice(h * dk, (h + 1) * dk)
                st_ref[2 * p + h] = st[h] * dec_all[:, cols] + lax.dot_general(
                    v[:, half], qk_all[3][:, cols], TN, preferred_element_type=F32)
                oh = o[:, half]
                y = oh * lax.rsqrt(jnp.mean(oh * oh, -1, keepdims=True) + NORM_EPS)
                gate = norm_g[:, cols] * _silu(g_ref[rows, cols].astype(F32))
                o_ref[rows, cols] = (y * gate).astype(o_ref.dtype)

    def body(n, prepared):
        consume(n, prepared)
        return prepare(jnp.minimum(n + 1, n_blocks - 1))

    lax.fori_loop(0, n_blocks, body, prepare(0), unroll=_unroll(n_blocks, (11, 3, 2, 1)))


def _hgrn(proj3, col_off, log_lb, log_1m_lb, norm_g, n_heads):
    b, l_pad, _ = proj3.shape
    sums, upper, pair = _hgrn_constants()
    heads = next(h for h in (4, 2) if n_heads % h == 0)
    groups = n_heads // heads
    width = heads * HGRN_DK
    base = col_off // width
    rb = _pick_div(l_pad, SEQ_ROWS_TARGET, HGRN_BLOCK)

    def col(section):
        return pl.BlockSpec((None, rb, width), lambda bi, hg, r: (bi, r, base + section * groups + hg))

    vec = pl.BlockSpec((1, width), lambda bi, hg, r: (0, hg))

    def const(shape):
        return pl.BlockSpec(shape, lambda bi, hg, r: (0,) * len(shape))

    return pl.pallas_call(
        functools.partial(_hgrn_kernel, n_blocks=rb // HGRN_BLOCK, heads=heads),
        grid=(b, groups, l_pad // rb),
        in_specs=[col(0), col(1), col(2), col(3), vec, vec, vec,
                  const(sums.shape), const(upper.shape), const(pair.shape)],
        out_specs=pl.BlockSpec((None, rb, width), lambda bi, hg, r: (bi, r, hg)),
        out_shape=jax.ShapeDtypeStruct((b, l_pad, n_heads * HGRN_DK), BF16),
        scratch_shapes=[pltpu.VMEM((heads, HGRN_DK, HGRN_DK), F32)],
        compiler_params=_params(3),
        name="hgrn2",
    )(proj3, proj3, proj3, proj3, log_lb, log_1m_lb, norm_g,
      jnp.asarray(sums, BF16), jnp.asarray(upper), jnp.asarray(pair, BF16))


def kernel(x, meta_tokens, emb_ln_g, emb_ln_b, hgrn_lb, w_in, b_gate, w_ret_out, w_hgrn_out, w_o,
           hgrn_norm_g, ln1_g, ln1_b, w_up, conv_w, conv_b, w_down, ln2_g, ln2_b):
    batch, seq, d = x.shape
    depth = w_in.shape[0]
    n_meta = meta_tokens.shape[0]
    l_valid = n_meta + seq
    l_pad = -(-l_valid // LANES) * LANES
    rows = batch * l_pad
    ret_w = w_ret_out.shape[1]
    hgrn_w = w_hgrn_out.shape[1]
    ret_heads = ret_w // RET_DK
    hgrn_heads = hgrn_w // HGRN_DK
    in_cols = w_in.shape[-1]
    assert in_cols == 4 * ret_w + 4 * hgrn_w + 2 * d
    alpha = float((2 * depth) ** 0.25)
    bm = _pick_div(l_pad, 1056, BF16_ROWS)

    meta = jnp.broadcast_to(meta_tokens[None].astype(x.dtype), (batch, n_meta, d))
    pad = jnp.zeros((batch, l_pad - l_valid, d), x.dtype)
    h0 = jnp.concatenate([meta, x, pad], axis=1).reshape(rows, d)
    xf, xb = _layer_norm(h0, emb_ln_g, emb_ln_b)

    pos = jnp.arange(l_pad, dtype=F32)
    half = RET_DK // 2
    inv = ROPE_BASE ** (-jnp.arange(half, dtype=F32) / half)
    ang = pos[:, None] * inv[None, :]
    cos, sin = jnp.cos(ang), jnp.sin(ang)
    log_gamma = jnp.log1p(-jnp.exp2(-5.0 - jnp.arange(ret_heads, dtype=F32)))

    lb_all = jnp.cumsum(jax.nn.softmax(hgrn_lb.astype(F32), axis=0), axis=0)
    lb_all = lb_all - lb_all[0:1]
    log_lb_all = jnp.log(lb_all)
    log_1m_lb_all = jnp.log1p(-lb_all)

    w_in_l = w_in[0:1].astype(BF16)

    for layer in range(depth):
        casts = [(w, layer) for w in (w_up, w_down, w_ret_out, w_hgrn_out, w_o)]
        casts += [(w_in, layer + 1)] if layer + 1 < depth else []
        proj, cast = _in_proj(xb, w_in_l, 0, bm, casts)
        w_up_l, w_down_l, w_ret_l, w_hgrn_l, w_o_l = cast[:5]
        w_in_l = cast[5] if layer + 1 < depth else None
        proj3 = proj.reshape(batch, l_pad, in_cols)
        ro = _retention(proj3, cos, sin, log_gamma, ret_heads).reshape(rows, ret_w)
        ho = _hgrn(proj3, 4 * ret_w, log_lb_all[layer:layer + 1], log_1m_lb_all[layer:layer + 1],
                   hgrn_norm_g[layer:layer + 1], hgrn_heads).reshape(rows, hgrn_w)
        merged = _merge(ro, ho, w_ret_l, w_hgrn_l, 0, proj, 4 * ret_w + 4 * hgrn_w,
                        b_gate[layer:layer + 1], bm)
        z1 = _matmul_resid(merged, w_o_l, 0, xf, alpha, bm)
        x1f, x1b = _layer_norm(z1, ln1_g[layer], ln1_b[layer])
        act = _ffn_up(x1b, w_up_l, 0, conv_w[layer], conv_b[layer:layer + 1], bm, l_pad)
        z2 = _matmul_resid(act, w_down_l, 0, x1f, alpha, bm)
        xf, xb = _layer_norm(z2, ln2_g[layer], ln2_b[layer])

    return xf.reshape(batch, l_pad, d)[:, n_meta:l_valid]
```
